```python
import math
import jax, jax.numpy as jnp
from jax import lax
import numpy as np

D_MODEL = 1024
BATCH = 2
SEQ = 16384
DEPTH = 4

MIX_WIDTH = D_MODEL
GM_GROUPS = 4
GM_DIM = MIX_WIDTH // 2 // GM_GROUPS
GM_W = GM_GROUPS * GM_DIM
GM_CHUNK = 128
ML_HEADS = 4
ML_DIM = MIX_WIDTH // 2 // ML_HEADS
ML_W = ML_HEADS * ML_DIM
ML_CHUNK = 128
CONV_WIDTH = 4
EV_IN = 2 * GM_W + 4 * ML_W + 2 * ML_HEADS

MLA_HEADS = 8
QK_NOPE = 128
QK_ROPE = 64
V_DIM = 128
Q_LORA = 512
KV_LORA = 256
ROPE_THETA = 10000.0
Q_BLOCK = 128
OD_IN = Q_LORA + KV_LORA + QK_ROPE

PEER_HEADS = 8
N_KEYS = 128
N_EXPERTS = N_KEYS * N_KEYS
PEER_TOPK = 16
PEER_QDIM = 256
PEER_HALF = PEER_QDIM // 2
PEER_BLOCK = 512

EPS = 1e-6

kernel_name = "hybrid_sgu_mlstm_mla_peer_trunk"


def rmsnorm(x, gain):
    xf = x.astype(jnp.float32)
    y = xf * lax.rsqrt(jnp.mean(xf * xf, axis=-1, keepdims=True) + EPS)
    return (y * gain.astype(jnp.float32)).astype(x.dtype)


def apply_rope(x, cos, sin):
    xf = x.astype(jnp.float32)
    half = QK_ROPE // 2
    x1, x2 = xf[..., :half], xf[..., half:]
    out = jnp.concatenate([x1 * cos - x2 * sin, x2 * cos + x1 * sin], axis=-1)
    return out.astype(x.dtype)


def causal_conv(z, w, b):
    S = z.shape[1]
    zp = jnp.pad(z, ((0, 0), (CONV_WIDTH - 1, 0), (0, 0)))
    out = b
    for j in range(CONV_WIDTH):
        out = out + w[j] * zp[:, j:j + S]
    return out


def chunked_spatial_gating(u, v, v_gain, ws, sb):
    B, S, _ = u.shape
    nc = S // GM_CHUNK
    vn = rmsnorm(v.reshape(B, nc, GM_CHUNK, GM_GROUPS, GM_DIM), v_gain.reshape(GM_GROUPS, GM_DIM))
    causal = jnp.tril(jnp.ones((GM_CHUNK, GM_CHUNK), dtype=bool))
    wm = jnp.where(causal[None], ws, jnp.zeros_like(ws))
    s = jnp.einsum('gts,bcsgd->bctgd', wm, vn) + sb.T[:, :, None]
    return u * s.reshape(B, S, GM_W)


def mlstm_chunkwise(q, k, v, log_i, log_f):
    B, H, S, D = q.shape
    nc = S // ML_CHUNK
    to_chunks = lambda a: jnp.moveaxis(a.reshape(B, H, nc, ML_CHUNK, *a.shape[3:]), 2, 0)
    qc, kc, vc = to_chunks(q), to_chunks(k), to_chunks(v)
    lic = to_chunks(log_i)
    bc = jnp.cumsum(to_chunks(log_f), axis=-1)
    causal = jnp.tril(jnp.ones((ML_CHUNK, ML_CHUNK), dtype=bool))
    neg_inf = jnp.float32(-jnp.inf)

    def step(carry, inp):
        C, n, m = carry
        qb, kb, vb, lib, bb = inp
        dmat = jnp.where(causal, bb[..., :, None] - bb[..., None, :] + lib[..., None, :], neg_inf)
        inter = bb + m[..., None]
        m_t = jnp.maximum(inter, jnp.max(dmat, axis=-1))
        w_intra = jnp.exp(dmat - m_t[..., None])
        w_inter = jnp.exp(inter - m_t)
        sm = jnp.einsum('bhtd,bhsd->bhts', qb, kb) * w_intra
        num = jnp.einsum('bhts,bhsd->bhtd', sm, vb) + w_inter[..., None] * jnp.einsum('bhtk,bhkv->bhtv', qb, C)
        den = jnp.sum(sm, axis=-1) + w_inter * jnp.einsum('bhtk,bhk->bht', qb, n)
        h = num / jnp.maximum(jnp.abs(den), jnp.exp(-m_t))[..., None]
        b_last = bb[..., -1]
        g = b_last[..., None] - bb + lib
        m_new = jnp.maximum(b_last + m, jnp.max(g, axis=-1))
        a = jnp.exp(b_last + m - m_new)
        w = jnp.exp(g - m_new[..., None])
        C_new = a[..., None, None] * C + jnp.einsum('bhs,bhsk,bhsv->bhkv', w, kb, vb)
        n_new = a[..., None] * n + jnp.einsum('bhs,bhsk->bhk', w, kb)
        return (C_new, n_new, m_new), h

    init = (jnp.zeros((B, H, D, D), jnp.float32), jnp.zeros((B, H, D), jnp.float32),
            jnp.zeros((B, H), jnp.float32))
    _, hs = lax.scan(step, init, (qc, kc, vc, lic, bc))
    return jnp.moveaxis(hs, 0, 2).reshape(B, H, S, D)


def sgu_mlstm_mixer(xn, w_in, gm_v_gain, gm_ws, gm_b, conv_w, conv_b, i_bias, f_bias, h_gain, w_out):
    B, S, _ = xn.shape
    proj = xn @ w_in
    o0 = 2 * GM_W
    u, vg = proj[..., :GM_W], proj[..., GM_W:o0]
    qk_pre = proj[..., o0:o0 + 2 * ML_W]
    v_m = proj[..., o0 + 2 * ML_W:o0 + 3 * ML_W]
    o_pre = proj[..., o0 + 3 * ML_W:o0 + 4 * ML_W]
    i_pre = proj[..., o0 + 4 * ML_W:o0 + 4 * ML_W + ML_HEADS]
    f_pre = proj[..., o0 + 4 * ML_W + ML_HEADS:]
    a_out = chunked_spatial_gating(jax.nn.gelu(u), jax.nn.gelu(vg), gm_v_gain, gm_ws, gm_b)
    qk = jax.nn.silu(causal_conv(qk_pre, conv_w, conv_b))
    heads = lambda t: jnp.transpose(t.reshape(B, S, ML_HEADS, ML_DIM).astype(jnp.float32), (0, 2, 1, 3))
    q = heads(qk[..., :ML_W]) * jnp.float32(ML_DIM ** -0.5)
    k = heads(qk[..., ML_W:])
    v = heads(v_m)
    log_i = jnp.transpose((i_pre + i_bias).astype(jnp.float32), (0, 2, 1))
    log_f = jnp.transpose(jax.nn.log_sigmoid((f_pre + f_bias).astype(jnp.float32)), (0, 2, 1))
    h = mlstm_chunkwise(q, k, v, log_i, log_f)
    h = rmsnorm(jnp.transpose(h, (0, 2, 1, 3)), h_gain.reshape(ML_HEADS, ML_DIM))
    b_out = jax.nn.sigmoid(o_pre) * h.reshape(B, S, ML_W).astype(xn.dtype)
    return jnp.concatenate([a_out, b_out], axis=-1) @ w_out


def mla_mixer(xn, w_in, q_gain, kv_gain, w_uq, w_ukv, w_out, cos, sin):
    B, S, _ = xn.shape
    proj = xn @ w_in
    c_q = rmsnorm(proj[..., :Q_LORA], q_gain)
    c_kv = rmsnorm(proj[..., Q_LORA:Q_LORA + KV_LORA], kv_gain)
    k_rope = apply_rope(proj[..., Q_LORA + KV_LORA:], cos, sin)
    q = (c_q @ w_uq).reshape(B, S, MLA_HEADS, QK_NOPE + QK_ROPE)
    q_nope = q[..., :QK_NOPE]
    q_rope = apply_rope(q[..., QK_NOPE:], cos[:, None, :], sin[:, None, :])
    kv = (c_kv @ w_ukv).reshape(B, S, MLA_HEADS, QK_NOPE + V_DIM)
    k_nope, v = kv[..., :QK_NOPE], kv[..., QK_NOPE:]
    nb = S // Q_BLOCK
    blocks = lambda t: jnp.moveaxis(t.reshape(B, nb, Q_BLOCK, *t.shape[2:]), 1, 0)
    scale = jnp.float32((QK_NOPE + QK_ROPE) ** -0.5)
    key_pos = jnp.arange(S)

    def attend(args):
        qn, qr, i = args
        s = jnp.einsum('bqhd,bkhd->bhqk', qn, k_nope) + jnp.einsum('bqhr,bkr->bhqk', qr, k_rope)
        q_pos = i * Q_BLOCK + jnp.arange(Q_BLOCK)
        s = jnp.where(key_pos[None, :] <= q_pos[:, None], s.astype(jnp.float32) * scale, -jnp.inf)
        p = jax.nn.softmax(s, axis=-1).astype(v.dtype)
        return jnp.einsum('bhqk,bkhd->bqhd', p, v)

    o = lax.map(attend, (blocks(q_nope), blocks(q_rope), jnp.arange(nb)))
    o = jnp.moveaxis(o, 0, 1).reshape(B, S, MLA_HEADS * V_DIM)
    return o @ w_out


def peer_ffn(xn, w_q, sub_keys, u_tab, v_tab):
    B, S, D = xn.shape
    n_tok = B * S
    blk = math.gcd(n_tok, PEER_BLOCK)
    xs = xn.reshape(n_tok // blk, blk, D)

    def retrieve(xb):
        q = (xb @ w_q).reshape(blk, PEER_HEADS, 2, PEER_HALF)
        s = jnp.einsum('thpd,hpnd->thpn', q, sub_keys).astype(jnp.float32)
        sv, si = lax.top_k(s, PEER_TOPK)
        cand_s = (sv[:, :, 0, :, None] + sv[:, :, 1, None, :]).reshape(blk, PEER_HEADS, PEER_TOPK * PEER_TOPK)
        cand_i = (si[:, :, 0, :, None] * N_KEYS + si[:, :, 1, None, :]).reshape(blk, PEER_HEADS, PEER_TOPK * PEER_TOPK)
        top_s, top_p = lax.top_k(cand_s, PEER_TOPK)
        e_idx = jnp.take_along_axis(cand_i, top_p, axis=-1)
        gate = jax.nn.softmax(top_s, axis=-1)
        ug = u_tab[e_idx]
        vg = v_tab[e_idx]
        act = jax.nn.gelu(jnp.einsum('thkd,td->thk', ug, xb).astype(jnp.float32))
        return jnp.einsum('thk,thkd->td', (gate * act).astype(xb.dtype), vg)

    return lax.map(retrieve, xs).reshape(B, S, D)


def setup_inputs(seed: int = 0) -> dict:
    key = jax.random.key(seed)
    ks = jax.random.split(key, 28)
    ne = (DEPTH + 1) // 2
    no = DEPTH // 2
    nrm = lambda k, shape, scale: jax.random.normal(k, shape, jnp.float32) * scale
    gain = lambda k, shape: 1.0 + 0.01 * jax.random.normal(k, shape, jnp.float32)
    return {
        "x": jax.random.normal(ks[0], (BATCH, SEQ, D_MODEL), jnp.float32),
        "norm_mix": gain(ks[1], (DEPTH, D_MODEL)),
        "norm_ffn": gain(ks[2], (DEPTH, D_MODEL)),
        "norm_final": gain(ks[3], (D_MODEL,)),
        "ev_w_in": nrm(ks[4], (ne, D_MODEL, EV_IN), D_MODEL ** -0.5),
        "ev_gm_v_gain": gain(ks[5], (ne, GM_W)),
        "ev_gm_ws": nrm(ks[6], (ne, GM_GROUPS, GM_CHUNK, GM_CHUNK), GM_CHUNK ** -0.5),
        "ev_gm_b": gain(ks[7], (ne, GM_GROUPS, GM_CHUNK)),
        "ev_conv_w": nrm(ks[8], (ne, CONV_WIDTH, 2 * ML_W), CONV_WIDTH ** -0.5),
        "ev_conv_b": nrm(ks[9], (ne, 2 * ML_W), 0.01),
        "ev_i_bias": nrm(ks[10], (ne, ML_HEADS), 0.01),
        "ev_f_bias": 3.0 + 0.5 * jax.random.normal(ks[11], (ne, ML_HEADS), jnp.float32),
        "ev_h_gain": gain(ks[12], (ne, ML_W)),
        "ev_w_out": nrm(ks[13], (ne, MIX_WIDTH, D_MODEL), MIX_WIDTH ** -0.5),
        "od_w_in": nrm(ks[14], (no, D_MODEL, OD_IN), D_MODEL ** -0.5),
        "od_q_gain": gain(ks[15], (no, Q_LORA)),
        "od_kv_gain": gain(ks[16], (no, KV_LORA)),
        "od_w_uq": nrm(ks[17], (no, Q_LORA, MLA_HEADS * (QK_NOPE + QK_ROPE)), Q_LORA ** -0.5),
        "od_w_ukv": nrm(ks[18], (no, KV_LORA, MLA_HEADS * (QK_NOPE + V_DIM)), KV_LORA ** -0.5),
        "od_w_out": nrm(ks[19], (no, MLA_HEADS * V_DIM, D_MODEL), (MLA_HEADS * V_DIM) ** -0.5),
        "peer_w_q": nrm(ks[20], (DEPTH, D_MODEL, PEER_HEADS * PEER_QDIM), D_MODEL ** -0.5),
        "peer_keys": nrm(ks[21], (DEPTH, PEER_HEADS, 2, N_KEYS, PEER_HALF), PEER_HALF ** -0.5),
        "peer_u": nrm(ks[22], (DEPTH, N_EXPERTS, D_MODEL), D_MODEL ** -0.5),
        "peer_v": nrm(ks[23], (DEPTH, N_EXPERTS, D_MODEL), (PEER_HEADS * PEER_TOPK) ** -0.5),
    }


def reference(x, norm_mix, norm_ffn, norm_final, ev_w_in, ev_gm_v_gain, ev_gm_ws, ev_gm_b,
              ev_conv_w, ev_conv_b, ev_i_bias, ev_f_bias, ev_h_gain, ev_w_out,
              od_w_in, od_q_gain, od_kv_gain, od_w_uq, od_w_ukv, od_w_out,
              peer_w_q, peer_keys, peer_u, peer_v):
    S = x.shape[1]
    pos = jnp.arange(S, dtype=jnp.float32)
    inv_freq = ROPE_THETA ** (-jnp.arange(QK_ROPE // 2, dtype=jnp.float32) / (QK_ROPE // 2))
    ang = pos[:, None] * inv_freq[None, :]
    cos, sin = jnp.cos(ang), jnp.sin(ang)
    for layer in range(DEPTH):
        j = layer // 2
        xn = rmsnorm(x, norm_mix[layer])
        if layer % 2 == 0:
            x = x + sgu_mlstm_mixer(xn, ev_w_in[j], ev_gm_v_gain[j], ev_gm_ws[j], ev_gm_b[j],
                                    ev_conv_w[j], ev_conv_b[j], ev_i_bias[j], ev_f_bias[j],
                                    ev_h_gain[j], ev_w_out[j])
        else:
            x = x + mla_mixer(xn, od_w_in[j], od_q_gain[j], od_kv_gain[j], od_w_uq[j],
                              od_w_ukv[j], od_w_out[j], cos, sin)
        x = x + peer_ffn(rmsnorm(x, norm_ffn[layer]), peer_w_q[layer], peer_keys[layer],
                         peer_u[layer], peer_v[layer])
    return rmsnorm(x, norm_final)
```

```python
import functools
import math

import jax
import jax.numpy as jnp
import numpy as np
from jax import lax
from jax.experimental import pallas as pl
from jax.experimental.pallas import tpu as pltpu

F32 = jnp.float32
BF16 = jnp.bfloat16
EPS = 1e-6
LANES = 128
VMEM_LIMIT = 56 * 1024 * 1024

GM_GROUPS = 4
GM_W = 512
ML_HEADS = 4
ML_W = 512
CHUNK = 128
CONV_WIDTH = 4
EV_IN = 2 * GM_W + 4 * ML_W + 2 * ML_HEADS
EV_IN_PAD = 2 * GM_W + 4 * ML_W + LANES
MLA_HEADS = 8
QK_NOPE = 128
QK_ROPE = 64
V_DIM = 128
Q_LORA = 512
KV_LORA = 256
ROPE_THETA = 10000.0
QK_PAD = 256
PEER_HEADS = 8
N_KEYS = 128
PEER_TOPK = 16
PEER_HALF = 128


def _params(*sem):
    return pltpu.CompilerParams(dimension_semantics=sem, vmem_limit_bytes=VMEM_LIMIT)


def _rms(x, gain):
    return x * lax.rsqrt(jnp.mean(x * x, axis=-1, keepdims=True) + EPS) * gain


def _gelu(x):
    c = math.sqrt(2.0 / math.pi)
    return x * (0.5 * (1.0 + jnp.tanh(c * (x + 0.044715 * (x * x * x)))))


def _sigmoid(x):
    return 1.0 / (1.0 + jnp.exp(-x))


def _dot(a, b):
    return jnp.dot(a, b, preferred_element_type=F32)


def _dot_nt(a, b):
    return lax.dot_general(a, b, (((1,), (1,)), ((), ())), preferred_element_type=F32)


def _dot_tn(a, b):
    return lax.dot_general(a, b, (((0,), (0,)), ((), ())), preferred_element_type=F32)


def _norm_matmul_kernel(x_ref, g_ref, w_ref, o_ref):
    xn = _rms(x_ref[...], g_ref[...])
    o_ref[...] = _dot(xn.astype(BF16), w_ref[...])


def norm_matmul(x, gain, w, tm=512):
    n, d = x.shape
    m = w.shape[1]
    tm = min(tm, n)
    return pl.pallas_call(
        _norm_matmul_kernel,
        grid=(n // tm,),
        in_specs=[pl.BlockSpec((tm, d), lambda i: (i, 0)),
                  pl.BlockSpec((1, d), lambda i: (0, 0)),
                  pl.BlockSpec((d, m), lambda i: (0, 0))],
        out_specs=pl.BlockSpec((tm, m), lambda i: (i, 0)),
        out_shape=jax.ShapeDtypeStruct((n, m), F32),
        compiler_params=_params("parallel"),
        name="norm_matmul",
    )(x, gain.reshape(1, d), w)


def _matmul_residual_kernel(a_ref, w_ref, x_ref, o_ref):
    o_ref[...] = x_ref[...] + _dot(a_ref[...], w_ref[...])


def matmul_residual(a, w, x, tm=512):
    n, k = a.shape
    d = w.shape[1]
    tm = min(tm, n)
    return pl.pallas_call(
        _matmul_residual_kernel,
        grid=(n // tm,),
        in_specs=[pl.BlockSpec((tm, k), lambda i: (i, 0)),
                  pl.BlockSpec((k, d), lambda i: (0, 0)),
                  pl.BlockSpec((tm, d), lambda i: (i, 0))],
        out_specs=pl.BlockSpec((tm, d), lambda i: (i, 0)),
        out_shape=jax.ShapeDtypeStruct((n, d), F32),
        compiler_params=_params("parallel"),
        name="matmul_residual",
    )(a, w, x)


def _even_mixer_kernel(proj_ref, x_ref, vgain_ref, ws_ref, sbt_ref, convw_ref, convb_ref,
                       gbias_ref, hgain_ref, wout_ref, o_ref,
                       zbuf, cext, mstate, mixed, *, n_chunks):
    @pl.when(pl.program_id(1) == 0)
    def _():
        zbuf[0:8, :] = jnp.zeros((8, 2 * ML_W), F32)
        cext[...] = jnp.zeros(cext.shape, F32)
        mstate[...] = jnp.zeros(mstate.shape, F32)

    row = lax.broadcasted_iota(jnp.int32, (CHUNK, CHUNK), 0)
    col = lax.broadcasted_iota(jnp.int32, (CHUNK, CHUNK), 1)
    causal = col <= row
    tri = jnp.where(causal, 1.0, 0.0).astype(F32)
    ones_col = jnp.where(col == 0, 1.0, 0.0).astype(F32)
    q_scale = float(CHUNK) ** -0.5

    for c in range(n_chunks):
        rows = slice(c * CHUNK, (c + 1) * CHUNK)

        for g in range(GM_GROUPS):
            lanes = slice(g * LANES, (g + 1) * LANES)
            u = _gelu(proj_ref[0, rows, lanes])
            v = _gelu(proj_ref[0, rows, GM_W + g * LANES:GM_W + (g + 1) * LANES])
            vn = _rms(v, vgain_ref[:, lanes])
            wm = jnp.where(causal, ws_ref[g], 0.0)
            s = _dot(wm.astype(BF16), vn.astype(BF16)) + sbt_ref[:, g:g + 1]
            mixed[rows, lanes] = (u * s).astype(BF16)

        zbuf[8:8 + CHUNK, :] = proj_ref[0, rows, 2 * GM_W:2 * GM_W + 2 * ML_W]
        conv = convb_ref[...]
        for j in range(CONV_WIDTH):
            lo = 8 - (CONV_WIDTH - 1) + j
            conv = conv + convw_ref[j:j + 1, :] * zbuf[lo:lo + CHUNK, :]
        zbuf[0:8, :] = zbuf[CHUNK:CHUNK + 8, :]
        qk = conv * _sigmoid(conv)

        gt = proj_ref[0, rows, EV_IN_PAD - LANES:EV_IN_PAD] + gbias_ref[...]
        logf = jnp.minimum(gt, 0.0) - jnp.log1p(jnp.exp(-jnp.abs(gt)))
        bcum = jnp.dot(tri, logf, preferred_element_type=F32, precision=lax.Precision.HIGHEST)
        gt_t = gt.T
        bcum_t = bcum.T

        for hd in range(ML_HEADS):
            lanes = slice(hd * LANES, (hd + 1) * LANES)
            q = (qk[:, lanes] * q_scale).astype(BF16)
            kf = qk[:, ML_W + hd * LANES:ML_W + (hd + 1) * LANES]
            k = kf.astype(BF16)
            v = proj_ref[0, rows, 2 * GM_W + 2 * ML_W + hd * LANES:2 * GM_W + 2 * ML_W + (hd + 1) * LANES]
            o_pre = proj_ref[0, rows, 2 * GM_W + 3 * ML_W + hd * LANES:2 * GM_W + 3 * ML_W + (hd + 1) * LANES]
            b_col = bcum[:, ML_HEADS + hd:ML_HEADS + hd + 1]
            b_row = bcum_t[ML_HEADS + hd:ML_HEADS + hd + 1, :]
            i_col = gt[:, hd:hd + 1]
            i_row = gt_t[hd:hd + 1, :]
            m_prev = mstate[hd][:, 0:1]
            c_prev = cext[hd]

            dmat = jnp.where(causal, b_col - b_row + i_row, -jnp.inf)
            inter = b_col + m_prev
            m_t = jnp.maximum(inter, jnp.max(dmat, axis=-1, keepdims=True))
            w_intra = jnp.exp(dmat - m_t)
            w_inter = jnp.exp(inter - m_t)
            sm = _dot_nt(q, k) * w_intra
            v_ext = jnp.concatenate([v, ones_col], axis=1)
            numden = _dot(sm.astype(BF16), v_ext.astype(BF16)) + w_inter * _dot(q, c_prev.astype(BF16))
            num = numden[:, :LANES]
            den = numden[:, LANES:LANES + 1]
            h = num / jnp.maximum(jnp.abs(den), jnp.exp(-m_t))

            b_last = bcum[CHUNK - 1:CHUNK, ML_HEADS + hd:ML_HEADS + hd + 1]
            gdec = b_last - b_col + i_col
            m_new = jnp.maximum(b_last + m_prev, jnp.max(gdec, axis=0, keepdims=True))
            decay = jnp.exp(b_last + m_prev - m_new)
            w_new = jnp.exp(gdec - m_new)
            cext[hd] = decay * c_prev + _dot(kf.T.astype(BF16), (w_new * v_ext).astype(BF16))
            mstate[hd] = jnp.broadcast_to(m_new, (1, LANES))

            hn = _rms(h, hgain_ref[:, lanes])
            mixed[rows, GM_W + hd * LANES:GM_W + (hd + 1) * LANES] = (_sigmoid(o_pre) * hn).astype(BF16)

    o_ref[0] = x_ref[0] + _dot(mixed[...], wout_ref[...])


def even_mixer(proj, x, vgain, ws, sbt, convw, convb, gbias, hgain, wout, tt=256):
    b, s, d = x.shape
    tt = min(tt, s)
    const = lambda shape: pl.BlockSpec(shape, lambda bi, ti: (0,) * len(shape))
    return pl.pallas_call(
        functools.partial(_even_mixer_kernel, n_chunks=tt // CHUNK),
        grid=(b, s // tt),
        in_specs=[pl.BlockSpec((1, tt, EV_IN_PAD), lambda bi, ti: (bi, ti, 0)),
                  pl.BlockSpec((1, tt, d), lambda bi, ti: (bi, ti, 0)),
                  const((1, GM_W)), const((GM_GROUPS, CHUNK, CHUNK)), const((CHUNK, GM_GROUPS)),
                  const((CONV_WIDTH, 2 * ML_W)), const((1, 2 * ML_W)), const((1, LANES)),
                  const((1, ML_W)), const((GM_W + ML_W, d))],
        out_specs=pl.BlockSpec((1, tt, d), lambda bi, ti: (bi, ti, 0)),
        out_shape=jax.ShapeDtypeStruct((b, s, d), F32),
        scratch_shapes=[pltpu.VMEM((CHUNK + 8, 2 * ML_W), F32),
                        pltpu.VMEM((ML_HEADS, CHUNK, 2 * LANES), F32),
                        pltpu.VMEM((ML_HEADS, 1, LANES), F32),
                        pltpu.VMEM((tt, GM_W + ML_W), BF16)],
        compiler_params=_params("arbitrary", "arbitrary"),
        name="even_mixer",
    )(proj, x, vgain, ws, sbt, convw, convb, gbias, hgain, wout)


def _mla_prep_kernel(x_ref, g_ref, win_ref, qg_ref, kvg_ref, wuq_ref, wukv_ref, cs_ref,
                     q_ref, k_ref, v_ref):
    xn = _rms(x_ref[0], g_ref[...])
    proj = _dot(xn.astype(BF16), win_ref[...])
    c_q = _rms(proj[:, :Q_LORA], qg_ref[...])
    c_kv = _rms(proj[:, Q_LORA:Q_LORA + KV_LORA], kvg_ref[...])
    cs = cs_ref[...]
    tm = cs.shape[0]
    lane = lax.broadcasted_iota(jnp.int32, (tm, LANES), 1)
    t = proj[:, Q_LORA + KV_LORA:] * cs
    k_rope = jnp.where(lane < QK_ROPE, t + pltpu.roll(t, QK_ROPE, axis=1), 0.0).astype(BF16)
    q_all = _dot(c_q.astype(BF16), wuq_ref[...])
    kv_all = _dot(c_kv.astype(BF16), wukv_ref[...])
    scale = float(QK_NOPE + QK_ROPE) ** -0.5
    for h in range(MLA_HEADS):
        o = h * QK_PAD
        qt = q_all[:, o + QK_NOPE:o + QK_PAD] * cs
        q_ref[0, h, :, 0:QK_NOPE] = (q_all[:, o:o + QK_NOPE] * scale).astype(BF16)
        q_ref[0, h, :, QK_NOPE:QK_PAD] = ((qt + pltpu.roll(qt, QK_ROPE, axis=1)) * scale).astype(BF16)
        k_ref[0, h, :, 0:QK_NOPE] = kv_all[:, o:o + QK_NOPE].astype(BF16)
        k_ref[0, h, :, QK_NOPE:QK_PAD] = k_rope
        v_ref[0, h] = kv_all[:, o + QK_NOPE:o + QK_PAD].astype(BF16)


def mla_prep(x, gain, win, qg, kvg, wuq, wukv, cs, tm=512):
    b, s, d = x.shape
    tm = min(tm, s)
    const = lambda shape: pl.BlockSpec(shape, lambda bi, ti: (0,) * len(shape))
    head_out = lambda w: pl.BlockSpec((1, MLA_HEADS, tm, w), lambda bi, ti: (bi, 0, ti, 0))
    return pl.pallas_call(
        _mla_prep_kernel,
        grid=(b, s // tm),
        in_specs=[pl.BlockSpec((1, tm, d), lambda bi, ti: (bi, ti, 0)),
                  const((1, d)), const(win.shape), const((1, Q_LORA)), const((1, KV_LORA)),
                  const(wuq.shape), const(wukv.shape),
                  pl.BlockSpec((tm, LANES), lambda bi, ti: (ti, 0))],
        out_specs=[head_out(QK_PAD), head_out(QK_PAD), head_out(V_DIM)],
        out_shape=[jax.ShapeDtypeStruct((b, MLA_HEADS, s, QK_PAD), BF16),
                   jax.ShapeDtypeStruct((b, MLA_HEADS, s, QK_PAD), BF16),
                   jax.ShapeDtypeStruct((b, MLA_HEADS, s, V_DIM), BF16)],
        compiler_params=_params("parallel", "parallel"),
        name="mla_prep",
    )(x, gain, win, qg, kvg, wuq, wukv, cs)


def _flash_kernel(qi_ref, ki_ref, q_ref, k_ref, v_ref, o_ref, m_sc, l_sc, acc_sc, *, blk):
    step = pl.program_id(1)
    qi = qi_ref[step]
    ki = ki_ref[step]

    @pl.when(ki == 0)
    def _():
        m_sc[...] = jnp.full(m_sc.shape, -jnp.inf, F32)
        l_sc[...] = jnp.zeros(l_sc.shape, F32)
        acc_sc[...] = jnp.zeros(acc_sc.shape, F32)

    def sweep(diagonal):
        def head(h, carry):
            s = _dot_nt(q_ref[0, h], k_ref[0, h])
            if diagonal:
                r = lax.broadcasted_iota(jnp.int32, (blk, blk), 0)
                c = lax.broadcasted_iota(jnp.int32, (blk, blk), 1)
                s = jnp.where(c <= r, s, -jnp.inf)
            m_prev = m_sc[h]
            m_new = jnp.maximum(m_prev, jnp.max(s, axis=1, keepdims=True))
            alpha = jnp.exp(m_prev - m_new)
            p = jnp.exp(s - m_new[:, 0:1])
            l_sc[h] = alpha * l_sc[h] + jnp.sum(p, axis=1, keepdims=True)
            acc_sc[h] = alpha * acc_sc[h] + _dot(p.astype(BF16), v_ref[0, h])
            m_sc[h] = m_new
            return carry
        lax.fori_loop(0, MLA_HEADS, head, 0)

    @pl.when(ki < qi)
    def _():
        sweep(False)

    @pl.when(ki == qi)
    def _():
        sweep(True)
        for h in range(MLA_HEADS):
            o_ref[0, :, h * V_DIM:(h + 1) * V_DIM] = (acc_sc[h] / l_sc[h]).astype(BF16)


def flash_attention(q, k, v, blk=512):
    b, nh, s, _ = q.shape
    blk = min(blk, s)
    nq = s // blk
    qi = np.concatenate([np.full(i + 1, i, np.int32) for i in range(nq)])
    ki = np.concatenate([np.arange(i + 1, dtype=np.int32) for i in range(nq)])
    grid_spec = pltpu.PrefetchScalarGridSpec(
        num_scalar_prefetch=2,
        grid=(b, len(qi)),
        in_specs=[pl.BlockSpec((1, nh, blk, QK_PAD), lambda bi, p, qi_r, ki_r: (bi, 0, qi_r[p], 0)),
                  pl.BlockSpec((1, nh, blk, QK_PAD), lambda bi, p, qi_r, ki_r: (bi, 0, ki_r[p], 0)),
                  pl.BlockSpec((1, nh, blk, V_DIM), lambda bi, p, qi_r, ki_r: (bi, 0, ki_r[p], 0))],
        out_specs=pl.BlockSpec((1, blk, nh * V_DIM), lambda bi, p, qi_r, ki_r: (bi, qi_r[p], 0)),
        scratch_shapes=[pltpu.VMEM((nh, blk, LANES), F32),
                        pltpu.VMEM((nh, blk, LANES), F32),
                        pltpu.VMEM((nh, blk, V_DIM), F32)],
    )
    return pl.pallas_call(
        functools.partial(_flash_kernel, blk=blk),
        grid_spec=grid_spec,
        out_shape=jax.ShapeDtypeStruct((b, s, nh * V_DIM), BF16),
        compiler_params=_params("arbitrary", "arbitrary"),
        name="mla_attention",
    )(jnp.asarray(qi), jnp.asarray(ki), q, k, v)


def _top16(e):
    sub = lax.broadcasted_iota(jnp.int32, (PEER_TOPK, e.shape[1]), 0)
    top = jnp.zeros((PEER_TOPK, e.shape[1]), F32)
    cur = e
    for r in range(PEER_TOPK):
        mk = jnp.max(cur, axis=0, keepdims=True)
        top = jnp.where(sub == r, mk, top)
        if r + 1 < PEER_TOPK:
            cur = jnp.where(cur == mk, -1.0, cur)
    return jnp.maximum(top, 0.0)


def _pair_candidates(ea, eb):
    parts = [ea[0:1] * eb]
    parts += [ea[p:p + 1] * eb[0:8] for p in range(1, 8)]
    parts += [ea[8:16] * eb[0:1]]
    return jnp.concatenate(parts, axis=0)


def _router_kernel(x_ref, g_ref, wq_ref, keys_ref, xn_ref, e1_ref, e2_ref, tau_ref):
    xn = _rms(x_ref[...], g_ref[...]).astype(BF16)
    xn_ref[...] = xn
    q = _dot(xn, wq_ref[...]).astype(BF16)
    for h in range(PEER_HEADS):
        es, tops = [], []
        for half in range(2):
            j = 2 * h + half
            s_t = _dot_nt(keys_ref[j], q[:, j * PEER_HALF:(j + 1) * PEER_HALF])
            e = jnp.exp(s_t - jnp.max(s_t, axis=0, keepdims=True))
            es.append(e)
            tops.append(_top16(e))
        ea, eb = tops
        cand = _pair_candidates(ea, eb)
        cur = cand
        for r in range(PEER_TOPK):
            kth = jnp.max(cur, axis=0, keepdims=True)
            if r + 1 < PEER_TOPK:
                cur = jnp.where(cur == kth, -1.0, cur)
        sel = cand >= jnp.maximum(kth, 1e-30)
        z = jnp.sum(jnp.where(sel, cand, 0.0), axis=0, keepdims=True)
        inv_z = 1.0 / z
        cand_z = _pair_candidates(ea * inv_z, eb)
        tau_ref[h:h + 1, :] = jnp.min(jnp.where(sel, cand_z, jnp.inf), axis=0, keepdims=True)
        e1_ref[h] = es[0] * inv_z
        e2_ref[h] = es[1]


def peer_router(x, gain, wq, keys, tr=256):
    n, d = x.shape
    tr = min(tr, n)
    return pl.pallas_call(
        _router_kernel,
        grid=(n // tr,),
        in_specs=[pl.BlockSpec((tr, d), lambda i: (i, 0)),
                  pl.BlockSpec((1, d), lambda i: (0, 0)),
                  pl.BlockSpec(wq.shape, lambda i: (0, 0)),
                  pl.BlockSpec(keys.shape, lambda i: (0, 0, 0))],
        out_specs=[pl.BlockSpec((tr, d), lambda i: (i, 0)),
                   pl.BlockSpec((PEER_HEADS, N_KEYS, tr), lambda i: (0, 0, i)),
                   pl.BlockSpec((PEER_HEADS, N_KEYS, tr), lambda i: (0, 0, i)),
                   pl.BlockSpec((PEER_HEADS, tr), lambda i: (0, i))],
        out_shape=[jax.ShapeDtypeStruct((n, d), BF16),
                   jax.ShapeDtypeStruct((PEER_HEADS, N_KEYS, n), F32),
                   jax.ShapeDtypeStruct((PEER_HEADS, N_KEYS, n), F32),
                   jax.ShapeDtypeStruct((PEER_HEADS, n), F32)],
        compiler_params=_params("parallel"),
        name="peer_router",
    )(x, gain.reshape(1, d), wq, keys)


def _peer_expert_kernel(xn_ref, x_ref, e1_ref, e2_ref, tau_ref, u_ref, v_ref, o_ref,
                        act_sc, w_sc, acc_sc, *, n_i, tc):
    e = pl.program_id(1)
    tb = xn_ref.shape[0]

    @pl.when(e == 0)
    def _():
        acc_sc[...] = jnp.zeros(acc_sc.shape, F32)

    act_sc[...] = _dot_nt(u_ref[...], xn_ref[...])

    def per_first_key(i, carry):
        ig = e * n_i + i
        r0 = pl.multiple_of(i * N_KEYS, N_KEYS)
        for c in range(tb // tc):
            cols = slice(c * tc, (c + 1) * tc)
            gates = jnp.zeros((N_KEYS, tc), F32)
            for h in range(PEER_HEADS):
                p = e2_ref[h, :, cols] * e1_ref[h, pl.ds(ig, 1), cols]
                gates = gates + jnp.where(p >= tau_ref[h:h + 1, cols], p, 0.0)
            a = act_sc[pl.ds(r0, N_KEYS), cols]
            w_sc[pl.ds(r0, N_KEYS), cols] = (gates * _gelu(a)).astype(BF16)
        return carry

    lax.fori_loop(0, n_i, per_first_key, 0)
    acc_sc[...] += _dot_tn(w_sc[...], v_ref[...])

    @pl.when(e == pl.num_programs(1) - 1)
    def _():
        o_ref[...] = x_ref[...] + acc_sc[...]


def peer_experts(xn, x, e1, e2, tau, u, v, tb=512, et=1024, tc=256):
    n, d = x.shape
    ne = u.shape[0]
    tb = min(tb, n)
    tc = min(tc, tb)
    return pl.pallas_call(
        functools.partial(_peer_expert_kernel, n_i=et // N_KEYS, tc=tc),
        grid=(n // tb, ne // et),
        in_specs=[pl.BlockSpec((tb, d), lambda i, j: (i, 0)),
                  pl.BlockSpec((tb, d), lambda i, j: (i, 0)),
                  pl.BlockSpec((PEER_HEADS, N_KEYS, tb), lambda i, j: (0, 0, i)),
                  pl.BlockSpec((PEER_HEADS, N_KEYS, tb), lambda i, j: (0, 0, i)),
                  pl.BlockSpec((PEER_HEADS, tb), lambda i, j: (0, i)),
                  pl.BlockSpec((et, d), lambda i, j: (j, 0)),
                  pl.BlockSpec((et, d), lambda i, j: (j, 0))],
        out_specs=pl.BlockSpec((tb, d), lambda i, j: (i, 0)),
        out_shape=jax.ShapeDtypeStruct((n, d), F32),
        scratch_shapes=[pltpu.VMEM((et, tb), F32),
                        pltpu.VMEM((et, tb), BF16),
                        pltpu.VMEM((tb, d), F32)],
        compiler_params=_params("parallel", "arbitrary"),
        name="peer_experts",
    )(xn, x, e1, e2, tau, u, v)


def _final_norm_kernel(x_ref, g_ref, o_ref):
    o_ref[...] = _rms(x_ref[...], g_ref[...])


def final_norm(x, gain, tm=1024):
    n, d = x.shape
    tm = min(tm, n)
    return pl.pallas_call(
        _final_norm_kernel,
        grid=(n // tm,),
        in_specs=[pl.BlockSpec((tm, d), lambda i: (i, 0)), pl.BlockSpec((1, d), lambda i: (0, 0))],
        out_specs=pl.BlockSpec((tm, d), lambda i: (i, 0)),
        out_shape=jax.ShapeDtypeStruct((n, d), F32),
        compiler_params=_params("parallel"),
        name="final_norm",
    )(x, gain.reshape(1, d))


def _swap_halves(w):
    half = w.shape[-1] // 2
    return jnp.concatenate([w[..., half:], w[..., :half]], axis=-1)


def _even_layer(x, gain, w_in, v_gain, ws, sb, conv_w, conv_b, i_bias, f_bias, h_gain, w_out):
    b, s, d = x.shape
    w_in_p = jnp.pad(w_in, ((0, 0), (0, EV_IN_PAD - EV_IN))).astype(BF16)
    proj = norm_matmul(x.reshape(b * s, d), gain, w_in_p).reshape(b, s, EV_IN_PAD)
    gbias = jnp.pad(jnp.concatenate([i_bias, f_bias]), (0, LANES - 2 * ML_HEADS)).reshape(1, LANES)
    return even_mixer(proj, x, v_gain.reshape(1, GM_W), ws, sb.T, conv_w, conv_b.reshape(1, 2 * ML_W),
                      gbias, h_gain.reshape(1, ML_W), w_out.astype(BF16))


def _odd_layer(x, gain, w_in, q_gain, kv_gain, w_uq, w_ukv, w_out, cs):
    b, s, d = x.shape
    rope_cols = w_in[:, Q_LORA + KV_LORA:]
    win = jnp.concatenate([w_in, _swap_halves(rope_cols)], axis=1).astype(BF16)
    wq = w_uq.reshape(Q_LORA, MLA_HEADS, QK_NOPE + QK_ROPE)
    wuq = jnp.concatenate([wq, _swap_halves(wq[..., QK_NOPE:])], axis=-1)
    wuq = wuq.reshape(Q_LORA, MLA_HEADS * QK_PAD).astype(BF16)
    q, k, v = mla_prep(x, gain.reshape(1, d), win, q_gain.reshape(1, Q_LORA), kv_gain.reshape(1, KV_LORA),
                       wuq, w_ukv.astype(BF16), cs)
    o = flash_attention(q, k, v)
    return matmul_residual(o.reshape(b * s, MLA_HEADS * V_DIM), w_out.astype(BF16),
                           x.reshape(b * s, d)).reshape(b, s, d)


def _peer_layer(x, gain, w_q, keys, u_tab, v_tab):
    b, s, d = x.shape
    x2 = x.reshape(b * s, d)
    keys16 = keys.reshape(PEER_HEADS * 2, N_KEYS, PEER_HALF).astype(BF16)
    xn, e1, e2, tau = peer_router(x2, gain, w_q.astype(BF16), keys16)
    out = peer_experts(xn, x2, e1, e2, tau, u_tab.astype(BF16), v_tab.astype(BF16))
    return out.reshape(b, s, d)


def kernel(x, norm_mix, norm_ffn, norm_final, ev_w_in, ev_gm_v_gain, ev_gm_ws, ev_gm_b, ev_conv_w, ev_conv_b, ev_i_bias, ev_f_bias, ev_h_gain, ev_w_out, od_w_in, od_q_gain, od_kv_gain, od_w_uq, od_w_ukv, od_w_out, peer_w_q, peer_keys, peer_u, peer_v):
    b, s, d = x.shape
    depth = norm_mix.shape[0]
    pos = jnp.arange(s, dtype=F32)
    inv_freq = ROPE_THETA ** (-jnp.arange(QK_ROPE // 2, dtype=F32) / (QK_ROPE // 2))
    ang = pos[:, None] * inv_freq[None, :]
    cos, sin = jnp.cos(ang), jnp.sin(ang)
    cs = jnp.concatenate([cos, cos, -sin, sin], axis=1)
    for layer in range(depth):
        j = layer // 2
        if layer % 2 == 0:
            x = _even_layer(x, norm_mix[layer], ev_w_in[j], ev_gm_v_gain[j], ev_gm_ws[j], ev_gm_b[j],
                            ev_conv_w[j], ev_conv_b[j], ev_i_bias[j], ev_f_bias[j], ev_h_gain[j], ev_w_out[j])
        else:
            x = _odd_layer(x, norm_mix[layer], od_w_in[j], od_q_gain[j], od_kv_gain[j], od_w_uq[j],
                           od_w_ukv[j], od_w_out[j], cs)
        x = _peer_layer(x, norm_ffn[layer], peer_w_q[layer], peer_keys[layer], peer_u[layer], peer_v[layer])
    return final_norm(x.reshape(b * s, d), norm_final).reshape(b, s, d)
```

```python
import functools
import math

import jax
import jax.numpy as jnp
import numpy as np
from jax import lax
from jax.experimental import pallas as pl
from jax.experimental.pallas import tpu as pltpu

F32 = jnp.float32
BF16 = jnp.bfloat16
EPS = 1e-6
LANES = 128
VMEM_LIMIT = 56 * 1024 * 1024

GM_GROUPS = 4
GM_W = 512
ML_HEADS = 4
ML_W = 512
CHUNK = 128
CONV_WIDTH = 4
EV_IN = 2 * GM_W + 4 * ML_W + 2 * ML_HEADS
EV_IN_PAD = 2 * GM_W + 4 * ML_W + LANES
MLA_HEADS = 8
QK_NOPE = 128
QK_ROPE = 64
V_DIM = 128
Q_LORA = 512
KV_LORA = 256
ROPE_THETA = 10000.0
QK_PAD = 256
PEER_HEADS = 8
N_KEYS = 128
PEER_TOPK = 16
PEER_HALF = 128


def _params(*sem):
    return pltpu.CompilerParams(dimension_semantics=sem, vmem_limit_bytes=VMEM_LIMIT)


def _rms(x, gain):
    return x * lax.rsqrt(jnp.mean(x * x, axis=-1, keepdims=True) + EPS) * gain


def _gelu(x):
    c = math.sqrt(2.0 / math.pi)
    return x * (0.5 * (1.0 + jnp.tanh(c * (x + 0.044715 * (x * x * x)))))


def _sigmoid(x):
    return 1.0 / (1.0 + jnp.exp(-x))


def _dot(a, b):
    return jnp.dot(a, b, preferred_element_type=F32)


def _dot_nt(a, b):
    return lax.dot_general(a, b, (((1,), (1,)), ((), ())), preferred_element_type=F32)


def _dot_tn(a, b):
    return lax.dot_general(a, b, (((0,), (0,)), ((), ())), preferred_element_type=F32)


def _norm_matmul_kernel(x_ref, g_ref, w_ref, o_ref):
    xn = _rms(x_ref[...], g_ref[...])
    o_ref[...] = _dot(xn.astype(BF16), w_ref[...])


def norm_matmul(x, gain, w, tm=512):
    n, d = x.shape
    m = w.shape[1]
    tm = min(tm, n)
    return pl.pallas_call(
        _norm_matmul_kernel,
        grid=(n // tm,),
        in_specs=[pl.BlockSpec((tm, d), lambda i: (i, 0)),
                  pl.BlockSpec((1, d), lambda i: (0, 0)),
                  pl.BlockSpec((d, m), lambda i: (0, 0))],
        out_specs=pl.BlockSpec((tm, m), lambda i: (i, 0)),
        out_shape=jax.ShapeDtypeStruct((n, m), F32),
        compiler_params=_params("parallel"),
        name="norm_matmul",
    )(x, gain.reshape(1, d), w)


def _matmul_residual_kernel(a_ref, w_ref, x_ref, o_ref):
    o_ref[...] = x_ref[...] + _dot(a_ref[...], w_ref[...])


def matmul_residual(a, w, x, tm=512):
    n, k = a.shape
    d = w.shape[1]
    tm = min(tm, n)
    return pl.pallas_call(
        _matmul_residual_kernel,
        grid=(n // tm,),
        in_specs=[pl.BlockSpec((tm, k), lambda i: (i, 0)),
                  pl.BlockSpec((k, d), lambda i: (0, 0)),
                  pl.BlockSpec((tm, d), lambda i: (i, 0))],
        out_specs=pl.BlockSpec((tm, d), lambda i: (i, 0)),
        out_shape=jax.ShapeDtypeStruct((n, d), F32),
        compiler_params=_params("parallel"),
        name="matmul_residual",
    )(a, w, x)


def _even_mixer_kernel(proj_ref, x_ref, vgain_ref, ws_ref, sbt_ref, convw_ref, convb_ref,
                       gbias_ref, hgain_ref, wout_ref, o_ref,
                       zbuf, cext, mstate, mixed, *, n_chunks):
    @pl.when(pl.program_id(1) == 0)
    def _():
        zbuf[0:8, :] = jnp.zeros((8, 2 * ML_W), F32)
        cext[...] = jnp.zeros(cext.shape, F32)
        mstate[...] = jnp.zeros(mstate.shape, F32)

    row = lax.broadcasted_iota(jnp.int32, (CHUNK, CHUNK), 0)
    col = lax.broadcasted_iota(jnp.int32, (CHUNK, CHUNK), 1)
    causal = col <= row
    tri = jnp.where(causal, 1.0, 0.0).astype(F32)
    ones_col = jnp.where(col == 0, 1.0, 0.0).astype(F32)
    q_scale = float(CHUNK) ** -0.5

    for c in range(n_chunks):
        rows = slice(c * CHUNK, (c + 1) * CHUNK)

        for g in range(GM_GROUPS):
            lanes = slice(g * LANES, (g + 1) * LANES)
            u = _gelu(proj_ref[0, rows, lanes])
            v = _gelu(proj_ref[0, rows, GM_W + g * LANES:GM_W + (g + 1) * LANES])
            vn = _rms(v, vgain_ref[:, lanes])
            wm = jnp.where(causal, ws_ref[g], 0.0)
            s = _dot(wm.astype(BF16), vn.astype(BF16)) + sbt_ref[:, g:g + 1]
            mixed[rows, lanes] = (u * s).astype(BF16)

        zbuf[8:8 + CHUNK, :] = proj_ref[0, rows, 2 * GM_W:2 * GM_W + 2 * ML_W]
        conv = convb_ref[...]
        for j in range(CONV_WIDTH):
            lo = 8 - (CONV_WIDTH - 1) + j
            conv = conv + convw_ref[j:j + 1, :] * zbuf[lo:lo + CHUNK, :]
        zbuf[0:8, :] = zbuf[CHUNK:CHUNK + 8, :]
        qk = conv * _sigmoid(conv)

        gt = proj_ref[0, rows, EV_IN_PAD - LANES:EV_IN_PAD] + gbias_ref[...]
        logf = jnp.minimum(gt, 0.0) - jnp.log1p(jnp.exp(-jnp.abs(gt)))
        bcum = jnp.dot(tri, logf, preferred_element_type=F32, precision=lax.Precision.HIGHEST)
        gt_t = gt.T
        bcum_t = bcum.T

        for hd in range(ML_HEADS):
            lanes = slice(hd * LANES, (hd + 1) * LANES)
            q = (qk[:, lanes] * q_scale).astype(BF16)
            kf = qk[:, ML_W + hd * LANES:ML_W + (hd + 1) * LANES]
            k = kf.astype(BF16)
            v = proj_ref[0, rows, 2 * GM_W + 2 * ML_W + hd * LANES:2 * GM_W + 2 * ML_W + (hd + 1) * LANES]
            o_pre = proj_ref[0, rows, 2 * GM_W + 3 * ML_W + hd * LANES:2 * GM_W + 3 * ML_W + (hd + 1) * LANES]
            b_col = bcum[:, ML_HEADS + hd:ML_HEADS + hd + 1]
            b_row = bcum_t[ML_HEADS + hd:ML_HEADS + hd + 1, :]
            i_col = gt[:, hd:hd + 1]
            i_row = gt_t[hd:hd + 1, :]
            m_prev = mstate[hd][:, 0:1]
            c_prev = cext[hd]

            dmat = jnp.where(causal, b_col - b_row + i_row, -jnp.inf)
            inter = b_col + m_prev
            m_t = jnp.maximum(inter, jnp.max(dmat, axis=-1, keepdims=True))
            w_intra = jnp.exp(dmat - m_t)
            w_inter = jnp.exp(inter - m_t)
            sm = _dot_nt(q, k) * w_intra
            v_ext = jnp.concatenate([v, ones_col], axis=1)
            numden = _dot(sm.astype(BF16), v_ext.astype(BF16)) + w_inter * _dot(q, c_prev.astype(BF16))
            num = numden[:, :LANES]
            den = numden[:, LANES:LANES + 1]
            h = num / jnp.maximum(jnp.abs(den), jnp.exp(-m_t))

            b_last = bcum[CHUNK - 1:CHUNK, ML_HEADS + hd:ML_HEADS + hd + 1]
            gdec = b_last - b_col + i_col
            m_new = jnp.maximum(b_last + m_prev, jnp.max(gdec, axis=0, keepdims=True))
            decay = jnp.exp(b_last + m_prev - m_new)
            w_new = jnp.exp(gdec - m_new)
            cext[hd] = decay * c_prev + _dot(kf.T.astype(BF16), (w_new * v_ext).astype(BF16))
            mstate[hd] = jnp.broadcast_to(m_new, (1, LANES))

            hn = _rms(h, hgain_ref[:, lanes])
            mixed[rows, GM_W + hd * LANES:GM_W + (hd + 1) * LANES] = (_sigmoid(o_pre) * hn).astype(BF16)

    o_ref[0] = x_ref[0] + _dot(mixed[...], wout_ref[...])


def even_mixer(proj, x, vgain, ws, sbt, convw, convb, gbias, hgain, wout, tt=256):
    b, s, d = x.shape
    tt = min(tt, s)
    const = lambda shape: pl.BlockSpec(shape, lambda bi, ti: (0,) * len(shape))
    return pl.pallas_call(
        functools.partial(_even_mixer_kernel, n_chunks=tt // CHUNK),
        grid=(b, s // tt),
        in_specs=[pl.BlockSpec((1, tt, EV_IN_PAD), lambda bi, ti: (bi, ti, 0)),
                  pl.BlockSpec((1, tt, d), lambda bi, ti: (bi, ti, 0)),
                  const((1, GM_W)), const((GM_GROUPS, CHUNK, CHUNK)), const((CHUNK, GM_GROUPS)),
                  const((CONV_WIDTH, 2 * ML_W)), const((1, 2 * ML_W)), const((1, LANES)),
                  const((1, ML_W)), const((GM_W + ML_W, d))],
        out_specs=pl.BlockSpec((1, tt, d), lambda bi, ti: (bi, ti, 0)),
        out_shape=jax.ShapeDtypeStruct((b, s, d), F32),
        scratch_shapes=[pltpu.VMEM((CHUNK + 8, 2 * ML_W), F32),
                        pltpu.VMEM((ML_HEADS, CHUNK, 2 * LANES), F32),
                        pltpu.VMEM((ML_HEADS, 1, LANES), F32),
                        pltpu.VMEM((tt, GM_W + ML_W), BF16)],
        compiler_params=_params("arbitrary", "arbitrary"),
        name="even_mixer",
    )(proj, x, vgain, ws, sbt, convw, convb, gbias, hgain, wout)


def _mla_prep_kernel(x_ref, g_ref, win_ref, qg_ref, kvg_ref, wuq_ref, wukv_ref, cs_ref,
                     q_ref, k_ref, v_ref):
    xn = _rms(x_ref[0], g_ref[...])
    proj = _dot(xn.astype(BF16), win_ref[...])
    c_q = _rms(proj[:, :Q_LORA], qg_ref[...])
    c_kv = _rms(proj[:, Q_LORA:Q_LORA + KV_LORA], kvg_ref[...])
    cs = cs_ref[...]
    tm = cs.shape[0]
    lane = lax.broadcasted_iota(jnp.int32, (tm, LANES), 1)
    t = proj[:, Q_LORA + KV_LORA:] * cs
    k_rope = jnp.where(lane < QK_ROPE, t + pltpu.roll(t, QK_ROPE, axis=1), 0.0).astype(BF16)
    q_all = _dot(c_q.astype(BF16), wuq_ref[...])
    kv_all = _dot(c_kv.astype(BF16), wukv_ref[...])
    scale = float(QK_NOPE + QK_ROPE) ** -0.5
    for h in range(MLA_HEADS):
        o = h * QK_PAD
        qt = q_all[:, o + QK_NOPE:o + QK_PAD] * cs
        q_ref[0, h, :, 0:QK_NOPE] = (q_all[:, o:o + QK_NOPE] * scale).astype(BF16)
        q_ref[0, h, :, QK_NOPE:QK_PAD] = ((qt + pltpu.roll(qt, QK_ROPE, axis=1)) * scale).astype(BF16)
        k_ref[0, h, :, 0:QK_NOPE] = kv_all[:, o:o + QK_NOPE].astype(BF16)
        k_ref[0, h, :, QK_NOPE:QK_PAD] = k_rope
        v_ref[0, h] = kv_all[:, o + QK_NOPE:o + QK_PAD].astype(BF16)


def mla_prep(x, gain, win, qg, kvg, wuq, wukv, cs, tm=512):
    b, s, d = x.shape
    tm = min(tm, s)
    const = lambda shape: pl.BlockSpec(shape, lambda bi, ti: (0,) * len(shape))
    head_out = lambda w: pl.BlockSpec((1, MLA_HEADS, tm, w), lambda bi, ti: (bi, 0, ti, 0))
    return pl.pallas_call(
        _mla_prep_kernel,
        grid=(b, s // tm),
        in_specs=[pl.BlockSpec((1, tm, d), lambda bi, ti: (bi, ti, 0)),
                  const((1, d)), const(win.shape), const((1, Q_LORA)), const((1, KV_LORA)),
                  const(wuq.shape), const(wukv.shape),
                  pl.BlockSpec((tm, LANES), lambda bi, ti: (ti, 0))],
        out_specs=[head_out(QK_PAD), head_out(QK_PAD), head_out(V_DIM)],
        out_shape=[jax.ShapeDtypeStruct((b, MLA_HEADS, s, QK_PAD), BF16),
                   jax.ShapeDtypeStruct((b, MLA_HEADS, s, QK_PAD), BF16),
                   jax.ShapeDtypeStruct((b, MLA_HEADS, s, V_DIM), BF16)],
        compiler_params=_params("parallel", "parallel"),
        name="mla_prep",
    )(x, gain, win, qg, kvg, wuq, wukv, cs)


def _flash_kernel(qi_ref, ki_ref, q_ref, k_ref, v_ref, o_ref, m_sc, l_sc, acc_sc, *, blk):
    step = pl.program_id(1)
    qi = qi_ref[step]
    ki = ki_ref[step]

    @pl.when(ki == 0)
    def _():
        m_sc[...] = jnp.full(m_sc.shape, -jnp.inf, F32)
        l_sc[...] = jnp.zeros(l_sc.shape, F32)
        acc_sc[...] = jnp.zeros(acc_sc.shape, F32)

    def sweep(diagonal):
        def head(h, carry):
            s = _dot_nt(q_ref[0, h], k_ref[0, h])
            if diagonal:
                r = lax.broadcasted_iota(jnp.int32, (blk, blk), 0)
                c = lax.broadcasted_iota(jnp.int32, (blk, blk), 1)
                s = jnp.where(c <= r, s, -jnp.inf)
            m_prev = m_sc[h]
            m_new = jnp.maximum(m_prev, jnp.max(s, axis=1, keepdims=True))
            alpha = jnp.exp(m_prev - m_new)
            p = jnp.exp(s - m_new[:, 0:1])
            l_sc[h] = alpha * l_sc[h] + jnp.sum(p, axis=1, keepdims=True)
            acc_sc[h] = alpha * acc_sc[h] + _dot(p.astype(BF16), v_ref[0, h])
            m_sc[h] = m_new
            return carry
        lax.fori_loop(0, MLA_HEADS, head, 0, unroll=True)

    @pl.when(ki < qi)
    def _():
        sweep(False)

    @pl.when(ki == qi)
    def _():
        sweep(True)
        for h in range(MLA_HEADS):
            o_ref[0, :, h * V_DIM:(h + 1) * V_DIM] = (acc_sc[h] / l_sc[h]).astype(BF16)


def flash_attention(q, k, v, blk=512):
    b, nh, s, _ = q.shape
    blk = min(blk, s)
    nq = s // blk
    qi = np.concatenate([np.full(i + 1, i, np.int32) for i in range(nq)])
    ki = np.concatenate([np.arange(i + 1, dtype=np.int32) for i in range(nq)])
    grid_spec = pltpu.PrefetchScalarGridSpec(
        num_scalar_prefetch=2,
        grid=(b, len(qi)),
        in_specs=[pl.BlockSpec((1, nh, blk, QK_PAD), lambda bi, p, qi_r, ki_r: (bi, 0, qi_r[p], 0)),
                  pl.BlockSpec((1, nh, blk, QK_PAD), lambda bi, p, qi_r, ki_r: (bi, 0, ki_r[p], 0)),
                  pl.BlockSpec((1, nh, blk, V_DIM), lambda bi, p, qi_r, ki_r: (bi, 0, ki_r[p], 0))],
        out_specs=pl.BlockSpec((1, blk, nh * V_DIM), lambda bi, p, qi_r, ki_r: (bi, qi_r[p], 0)),
        scratch_shapes=[pltpu.VMEM((nh, blk, LANES), F32),
                        pltpu.VMEM((nh, blk, LANES), F32),
                        pltpu.VMEM((nh, blk, V_DIM), F32)],
    )
    return pl.pallas_call(
        functools.partial(_flash_kernel, blk=blk),
        grid_spec=grid_spec,
        out_shape=jax.ShapeDtypeStruct((b, s, nh * V_DIM), BF16),
        compiler_params=_params("arbitrary", "arbitrary"),
        name="mla_attention",
    )(jnp.asarray(qi), jnp.asarray(ki), q, k, v)


def _top16(e):
    sub = lax.broadcasted_iota(jnp.int32, (PEER_TOPK, e.shape[1]), 0)
    top = jnp.zeros((PEER_TOPK, e.shape[1]), F32)
    cur = e
    for r in range(PEER_TOPK):
        mk = jnp.max(cur, axis=0, keepdims=True)
        top = jnp.where(sub == r, mk, top)
        if r + 1 < PEER_TOPK:
            cur = jnp.where(cur == mk, -1.0, cur)
    return jnp.maximum(top, 0.0)


def _pair_candidates(ea, eb):
    parts = [ea[0:1] * eb]
    parts += [ea[p:p + 1] * eb[0:8] for p in range(1, 8)]
    parts += [ea[8:16] * eb[0:1]]
    return jnp.concatenate(parts, axis=0)


def _router_kernel(x_ref, g_ref, wq_ref, keys_ref, xn_ref, e1_ref, e2_ref, tau_ref):
    xn = _rms(x_ref[...], g_ref[...]).astype(BF16)
    xn_ref[...] = xn
    q = _dot(xn, wq_ref[...]).astype(BF16)
    for h in range(PEER_HEADS):
        es, tops = [], []
        for half in range(2):
            j = 2 * h + half
            s_t = _dot_nt(keys_ref[j], q[:, j * PEER_HALF:(j + 1) * PEER_HALF])
            e = jnp.exp(s_t - jnp.max(s_t, axis=0, keepdims=True))
            es.append(e)
            tops.append(_top16(e))
        ea, eb = tops
        cand = _pair_candidates(ea, eb)
        cur = cand
        for r in range(PEER_TOPK):
            kth = jnp.max(cur, axis=0, keepdims=True)
            if r + 1 < PEER_TOPK:
                cur = jnp.where(cur == kth, -1.0, cur)
        sel = cand >= jnp.maximum(kth, 1e-30)
        z = jnp.sum(jnp.where(sel, cand, 0.0), axis=0, keepdims=True)
        inv_z = 1.0 / z
        cand_z = _pair_candidates(ea * inv_z, eb)
        tau_ref[h:h + 1, :] = jnp.min(jnp.where(sel, cand_z, jnp.inf), axis=0, keepdims=True)
        e1_ref[h] = es[0] * inv_z
        e2_ref[h] = es[1]


def peer_router(x, gain, wq, keys, tr=256):
    n, d = x.shape
    tr = min(tr, n)
    return pl.pallas_call(
        _router_kernel,
        grid=(n // tr,),
        in_specs=[pl.BlockSpec((tr, d), lambda i: (i, 0)),
                  pl.BlockSpec((1, d), lambda i: (0, 0)),
                  pl.BlockSpec(wq.shape, lambda i: (0, 0)),
                  pl.BlockSpec(keys.shape, lambda i: (0, 0, 0))],
        out_specs=[pl.BlockSpec((tr, d), lambda i: (i, 0)),
                   pl.BlockSpec((PEER_HEADS, N_KEYS, tr), lambda i: (0, 0, i)),
                   pl.BlockSpec((PEER_HEADS, N_KEYS, tr), lambda i: (0, 0, i)),
                   pl.BlockSpec((PEER_HEADS, tr), lambda i: (0, i))],
        out_shape=[jax.ShapeDtypeStruct((n, d), BF16),
                   jax.ShapeDtypeStruct((PEER_HEADS, N_KEYS, n), F32),
                   jax.ShapeDtypeStruct((PEER_HEADS, N_KEYS, n), F32),
                   jax.ShapeDtypeStruct((PEER_HEADS, n), F32)],
        compiler_params=_params("parallel"),
        name="peer_router",
    )(x, gain.reshape(1, d), wq, keys)


def _peer_expert_kernel(xn_ref, x_ref, e1_ref, e2_ref, tau_ref, u_ref, v_ref, o_ref,
                        acc_sc, act_sc, w_sc, *, et, n_sub, tc):
    e = pl.program_id(1)
    tb = xn_ref.shape[0]
    n_i = et // N_KEYS

    @pl.when(e == 0)
    def _():
        acc_sc[...] = jnp.zeros(acc_sc.shape, F32)

    mxu_w = 2 * LANES

    def rows(s):
        return slice(s * et, (s + 1) * et)

    def act_pieces(s):
        def piece(c):
            cols = slice(c * mxu_w, (c + 1) * mxu_w)
            act_sc[s, :, cols] = _dot_nt(u_ref[rows(s), :], xn_ref[cols, :])
        return [functools.partial(piece, c) for c in range(tb // mxu_w)]

    def out_pieces(s):
        def piece(c):
            cols = slice(c * mxu_w, (c + 1) * mxu_w)
            acc_sc[:, cols] += _dot_tn(w_sc[s], v_ref[rows(s), cols])
        return [functools.partial(piece, c) for c in range(acc_sc.shape[1] // mxu_w)]

    def build_pieces(s):
        def piece(i, c):
            ig = (e * n_sub + s) * n_i + i
            r = slice(i * N_KEYS, (i + 1) * N_KEYS)
            cols = slice(c * tc, (c + 1) * tc)
            gates = jnp.zeros((N_KEYS, tc), F32)
            for h in range(PEER_HEADS):
                p = e2_ref[h, :, cols] * e1_ref[h, pl.ds(ig, 1), :][:, cols]
                gates = gates + jnp.where(p >= tau_ref[h:h + 1, cols], p, 0.0)
            w_sc[s, r, cols] = (gates * _gelu(act_sc[s, r, cols])).astype(BF16)
        return [functools.partial(piece, i, c) for i in range(n_i) for c in range(tb // tc)]

    def interleave(vector_work, matrix_work):
        done = 0
        for k, piece in enumerate(vector_work):
            piece()
            due = (k + 1) * len(matrix_work) // len(vector_work)
            for m in matrix_work[done:due]:
                m()
            done = due

    for m in act_pieces(0):
        m()
    for s in range(n_sub):
        matrix_work = (act_pieces(s + 1) if s + 1 < n_sub else []) + (out_pieces(s - 1) if s > 0 else [])
        interleave(build_pieces(s), matrix_work)
    for m in out_pieces(n_sub - 1):
        m()

    @pl.when(e == pl.num_programs(1) - 1)
    def _():
        o_ref[...] = x_ref[...] + acc_sc[...]


def peer_experts(xn, x, e1, e2, tau, u, v, tb=512, et=512, n_sub=4, tc=128):
    n, d = x.shape
    ne = u.shape[0]
    tb = min(tb, n)
    tc = min(tc, tb)
    blk_e = et * n_sub
    return pl.pallas_call(
        functools.partial(_peer_expert_kernel, et=et, n_sub=n_sub, tc=tc),
        grid=(n // tb, ne // blk_e),
        in_specs=[pl.BlockSpec((tb, d), lambda i, j: (i, 0)),
                  pl.BlockSpec((tb, d), lambda i, j: (i, 0)),
                  pl.BlockSpec((PEER_HEADS, N_KEYS, tb), lambda i, j: (0, 0, i)),
                  pl.BlockSpec((PEER_HEADS, N_KEYS, tb), lambda i, j: (0, 0, i)),
                  pl.BlockSpec((PEER_HEADS, tb), lambda i, j: (0, i)),
                  pl.BlockSpec((blk_e, d), lambda i, j: (j, 0)),
                  pl.BlockSpec((blk_e, d), lambda i, j: (j, 0))],
        out_specs=pl.BlockSpec((tb, d), lambda i, j: (i, 0)),
        out_shape=jax.ShapeDtypeStruct((n, d), F32),
        scratch_shapes=[pltpu.VMEM((tb, d), F32),
                        pltpu.VMEM((n_sub, et, tb), F32),
                        pltpu.VMEM((n_sub, et, tb), BF16)],
        compiler_params=_params("parallel", "arbitrary"),
        name="peer_experts",
    )(xn, x, e1, e2, tau, u, v)


def _final_norm_kernel(x_ref, g_ref, o_ref):
    o_ref[...] = _rms(x_ref[...], g_ref[...])


def final_norm(x, gain, tm=1024):
    n, d = x.shape
    tm = min(tm, n)
    return pl.pallas_call(
        _final_norm_kernel,
        grid=(n // tm,),
        in_specs=[pl.BlockSpec((tm, d), lambda i: (i, 0)), pl.BlockSpec((1, d), lambda i: (0, 0))],
        out_specs=pl.BlockSpec((tm, d), lambda i: (i, 0)),
        out_shape=jax.ShapeDtypeStruct((n, d), F32),
        compiler_params=_params("parallel"),
        name="final_norm",
    )(x, gain.reshape(1, d))


def _swap_halves(w):
    half = w.shape[-1] // 2
    return jnp.concatenate([w[..., half:], w[..., :half]], axis=-1)


def _even_layer(x, gain, w_in, v_gain, ws, sb, conv_w, conv_b, i_bias, f_bias, h_gain, w_out):
    b, s, d = x.shape
    w_in_p = jnp.pad(w_in, ((0, 0), (0, EV_IN_PAD - EV_IN))).astype(BF16)
    proj = norm_matmul(x.reshape(b * s, d), gain, w_in_p).reshape(b, s, EV_IN_PAD)
    gbias = jnp.pad(jnp.concatenate([i_bias, f_bias]), (0, LANES - 2 * ML_HEADS)).reshape(1, LANES)
    return even_mixer(proj, x, v_gain.reshape(1, GM_W), ws, sb.T, conv_w, conv_b.reshape(1, 2 * ML_W),
                      gbias, h_gain.reshape(1, ML_W), w_out.astype(BF16))


def _odd_layer(x, gain, w_in, q_gain, kv_gain, w_uq, w_ukv, w_out, cs):
    b, s, d = x.shape
    rope_cols = w_in[:, Q_LORA + KV_LORA:]
    win = jnp.concatenate([w_in, _swap_halves(rope_cols)], axis=1).astype(BF16)
    wq = w_uq.reshape(Q_LORA, MLA_HEADS, QK_NOPE + QK_ROPE)
    wuq = jnp.concatenate([wq, _swap_halves(wq[..., QK_NOPE:])], axis=-1)
    wuq = wuq.reshape(Q_LORA, MLA_HEADS * QK_PAD).astype(BF16)
    q, k, v = mla_prep(x, gain.reshape(1, d), win, q_gain.reshape(1, Q_LORA), kv_gain.reshape(1, KV_LORA),
                       wuq, w_ukv.astype(BF16), cs)
    o = flash_attention(q, k, v)
    return matmul_residual(o.reshape(b * s, MLA_HEADS * V_DIM), w_out.astype(BF16),
                           x.reshape(b * s, d)).reshape(b, s, d)


def _peer_layer(x, gain, w_q, keys, u_tab, v_tab):
    b, s, d = x.shape
    x2 = x.reshape(b * s, d)
    keys16 = keys.reshape(PEER_HEADS * 2, N_KEYS, PEER_HALF).astype(BF16)
    xn, e1, e2, tau = peer_router(x2, gain, w_q.astype(BF16), keys16)
    out = peer_experts(xn, x2, e1, e2, tau, u_tab.astype(BF16), v_tab.astype(BF16))
    return out.reshape(b, s, d)


def kernel(x, norm_mix, norm_ffn, norm_final, ev_w_in, ev_gm_v_gain, ev_gm_ws, ev_gm_b, ev_conv_w, ev_conv_b, ev_i_bias, ev_f_bias, ev_h_gain, ev_w_out, od_w_in, od_q_gain, od_kv_gain, od_w_uq, od_w_ukv, od_w_out, peer_w_q, peer_keys, peer_u, peer_v):
    b, s, d = x.shape
    depth = norm_mix.shape[0]
    pos = jnp.arange(s, dtype=F32)
    inv_freq = ROPE_THETA ** (-jnp.arange(QK_ROPE // 2, dtype=F32) / (QK_ROPE // 2))
    ang = pos[:, None] * inv_freq[None, :]
    cos, sin = jnp.cos(ang), jnp.sin(ang)
    cs = jnp.concatenate([cos, cos, -sin, sin], axis=1)
    for layer in range(depth):
        j = layer // 2
        if layer % 2 == 0:
            x = _even_layer(x, norm_mix[layer], ev_w_in[j], ev_gm_v_gain[j], ev_gm_ws[j], ev_gm_b[j],
                            ev_conv_w[j], ev_conv_b[j], ev_i_bias[j], ev_f_bias[j], ev_h_gain[j], ev_w_out[j])
        else:
            x = _odd_layer(x, norm_mix[layer], od_w_in[j], od_q_gain[j], od_kv_gain[j], od_w_uq[j],
                           od_w_ukv[j], od_w_out[j], cs)
        x = _peer_layer(x, norm_ffn[layer], peer_w_q[layer], peer_keys[layer], peer_u[layer], peer_v[layer])
    return final_norm(x.reshape(b * s, d), norm_final).reshape(b, s, d)
```

```python
import functools
import math

import jax
import jax.numpy as jnp
import numpy as np
from jax import lax
from jax.experimental import pallas as pl
from jax.experimental.pallas import tpu as pltpu

F32 = jnp.float32
BF16 = jnp.bfloat16
EPS = 1e-6
LANES = 128
VMEM_LIMIT = 56 * 1024 * 1024

GM_GROUPS = 4
GM_W = 512
ML_HEADS = 4
ML_W = 512
CHUNK = 128
CONV_WIDTH = 4
EV_IN = 2 * GM_W + 4 * ML_W + 2 * ML_HEADS
EV_IN_PAD = 2 * GM_W + 4 * ML_W + LANES
MLA_HEADS = 8
QK_NOPE = 128
QK_ROPE = 64
V_DIM = 128
Q_LORA = 512
KV_LORA = 256
ROPE_THETA = 10000.0
QK_PAD = 256
V_PAD = 256
PEER_HEADS = 8
N_KEYS = 128
PEER_TOPK = 16
PEER_HALF = 128


def _params(*sem):
    return pltpu.CompilerParams(dimension_semantics=sem, vmem_limit_bytes=VMEM_LIMIT)


def _rms(x, gain):
    return x * lax.rsqrt(jnp.mean(x * x, axis=-1, keepdims=True) + EPS) * gain


def _gelu(x):
    c = math.sqrt(2.0 / math.pi)
    return x * (0.5 * (1.0 + jnp.tanh(c * (x + 0.044715 * (x * x * x)))))


def _sigmoid(x):
    return 1.0 / (1.0 + jnp.exp(-x))


def _dot(a, b):
    return jnp.dot(a, b, preferred_element_type=F32)


def _dot_nt(a, b):
    return lax.dot_general(a, b, (((1,), (1,)), ((), ())), preferred_element_type=F32)


def _dot_tn(a, b):
    return lax.dot_general(a, b, (((0,), (0,)), ((), ())), preferred_element_type=F32)


def _norm_matmul_kernel(x_ref, g_ref, w_ref, o_ref):
    xn = _rms(x_ref[...], g_ref[...])
    o_ref[...] = _dot(xn.astype(BF16), w_ref[...])


def norm_matmul(x, gain, w, tm=512):
    n, d = x.shape
    m = w.shape[1]
    tm = min(tm, n)
    return pl.pallas_call(
        _norm_matmul_kernel,
        grid=(n // tm,),
        in_specs=[pl.BlockSpec((tm, d), lambda i: (i, 0)),
                  pl.BlockSpec((1, d), lambda i: (0, 0)),
                  pl.BlockSpec((d, m), lambda i: (0, 0))],
        out_specs=pl.BlockSpec((tm, m), lambda i: (i, 0)),
        out_shape=jax.ShapeDtypeStruct((n, m), F32),
        compiler_params=_params("parallel"),
        name="norm_matmul",
    )(x, gain.reshape(1, d), w)


def _matmul_residual_kernel(a_ref, w_ref, x_ref, o_ref):
    o_ref[...] = x_ref[...] + _dot(a_ref[...], w_ref[...])


def matmul_residual(a, w, x, tm=512):
    n, k = a.shape
    d = w.shape[1]
    tm = min(tm, n)
    return pl.pallas_call(
        _matmul_residual_kernel,
        grid=(n // tm,),
        in_specs=[pl.BlockSpec((tm, k), lambda i: (i, 0)),
                  pl.BlockSpec((k, d), lambda i: (0, 0)),
                  pl.BlockSpec((tm, d), lambda i: (i, 0))],
        out_specs=pl.BlockSpec((tm, d), lambda i: (i, 0)),
        out_shape=jax.ShapeDtypeStruct((n, d), F32),
        compiler_params=_params("parallel"),
        name="matmul_residual",
    )(a, w, x)


def _even_mixer_kernel(proj_ref, x_ref, vgain_ref, ws_ref, sbt_ref, convw_ref, convb_ref,
                       gbias_ref, hgain_ref, wout_ref, o_ref,
                       zbuf, cext, mstate, mixed, *, n_chunks):
    @pl.when(pl.program_id(1) == 0)
    def _():
        zbuf[0:8, :] = jnp.zeros((8, 2 * ML_W), F32)
        cext[...] = jnp.zeros(cext.shape, F32)
        mstate[...] = jnp.zeros(mstate.shape, F32)

    row = lax.broadcasted_iota(jnp.int32, (CHUNK, CHUNK), 0)
    col = lax.broadcasted_iota(jnp.int32, (CHUNK, CHUNK), 1)
    causal = col <= row
    tri = jnp.where(causal, 1.0, 0.0).astype(F32)
    ones_col = jnp.where(col == 0, 1.0, 0.0).astype(F32)
    q_scale = float(CHUNK) ** -0.5

    for c in range(n_chunks):
        rows = slice(c * CHUNK, (c + 1) * CHUNK)

        for g in range(GM_GROUPS):
            lanes = slice(g * LANES, (g + 1) * LANES)
            u = _gelu(proj_ref[0, rows, lanes])
            v = _gelu(proj_ref[0, rows, GM_W + g * LANES:GM_W + (g + 1) * LANES])
            vn = _rms(v, vgain_ref[:, lanes])
            wm = jnp.where(causal, ws_ref[g], 0.0)
            s = _dot(wm.astype(BF16), vn.astype(BF16)) + sbt_ref[:, g:g + 1]
            mixed[rows, lanes] = (u * s).astype(BF16)

        zbuf[8:8 + CHUNK, :] = proj_ref[0, rows, 2 * GM_W:2 * GM_W + 2 * ML_W]
        conv = convb_ref[...]
        for j in range(CONV_WIDTH):
            lo = 8 - (CONV_WIDTH - 1) + j
            conv = conv + convw_ref[j:j + 1, :] * zbuf[lo:lo + CHUNK, :]
        zbuf[0:8, :] = zbuf[CHUNK:CHUNK + 8, :]
        qk = conv * _sigmoid(conv)

        gt = proj_ref[0, rows, EV_IN_PAD - LANES:EV_IN_PAD] + gbias_ref[...]
        logf = jnp.minimum(gt, 0.0) - jnp.log1p(jnp.exp(-jnp.abs(gt)))
        bcum = jnp.dot(tri, logf, preferred_element_type=F32, precision=lax.Precision.HIGHEST)
        gt_t = gt.T
        bcum_t = bcum.T

        for hd in range(ML_HEADS):
            lanes = slice(hd * LANES, (hd + 1) * LANES)
            q = (qk[:, lanes] * q_scale).astype(BF16)
            kf = qk[:, ML_W + hd * LANES:ML_W + (hd + 1) * LANES]
            k = kf.astype(BF16)
            v = proj_ref[0, rows, 2 * GM_W + 2 * ML_W + hd * LANES:2 * GM_W + 2 * ML_W + (hd + 1) * LANES]
            o_pre = proj_ref[0, rows, 2 * GM_W + 3 * ML_W + hd * LANES:2 * GM_W + 3 * ML_W + (hd + 1) * LANES]
            b_col = bcum[:, ML_HEADS + hd:ML_HEADS + hd + 1]
            b_row = bcum_t[ML_HEADS + hd:ML_HEADS + hd + 1, :]
            i_col = gt[:, hd:hd + 1]
            i_row = gt_t[hd:hd + 1, :]
            m_prev = mstate[hd][:, 0:1]
            c_prev = cext[hd]

            dmat = jnp.where(causal, b_col - b_row + i_row, -jnp.inf)
            inter = b_col + m_prev
            m_t = jnp.maximum(inter, jnp.max(dmat, axis=-1, keepdims=True))
            w_intra = jnp.exp(dmat - m_t)
            w_inter = jnp.exp(inter - m_t)
            sm = _dot_nt(q, k) * w_intra
            v_ext = jnp.concatenate([v, ones_col], axis=1)
            numden = _dot(sm.astype(BF16), v_ext.astype(BF16)) + w_inter * _dot(q, c_prev.astype(BF16))
            num = numden[:, :LANES]
            den = numden[:, LANES:LANES + 1]
            h = num / jnp.maximum(jnp.abs(den), jnp.exp(-m_t))

            b_last = bcum[CHUNK - 1:CHUNK, ML_HEADS + hd:ML_HEADS + hd + 1]
            gdec = b_last - b_col + i_col
            m_new = jnp.maximum(b_last + m_prev, jnp.max(gdec, axis=0, keepdims=True))
            decay = jnp.exp(b_last + m_prev - m_new)
            w_new = jnp.exp(gdec - m_new)
            cext[hd] = decay * c_prev + _dot(kf.T.astype(BF16), (w_new * v_ext).astype(BF16))
            mstate[hd] = jnp.broadcast_to(m_new, (1, LANES))

            hn = _rms(h, hgain_ref[:, lanes])
            mixed[rows, GM_W + hd * LANES:GM_W + (hd + 1) * LANES] = (_sigmoid(o_pre) * hn).astype(BF16)

    o_ref[0] = x_ref[0] + _dot(mixed[...], wout_ref[...])


def even_mixer(proj, x, vgain, ws, sbt, convw, convb, gbias, hgain, wout, tt=256):
    b, s, d = x.shape
    tt = min(tt, s)
    const = lambda shape: pl.BlockSpec(shape, lambda bi, ti: (0,) * len(shape))
    return pl.pallas_call(
        functools.partial(_even_mixer_kernel, n_chunks=tt // CHUNK),
        grid=(b, s // tt),
        in_specs=[pl.BlockSpec((1, tt, EV_IN_PAD), lambda bi, ti: (bi, ti, 0)),
                  pl.BlockSpec((1, tt, d), lambda bi, ti: (bi, ti, 0)),
                  const((1, GM_W)), const((GM_GROUPS, CHUNK, CHUNK)), const((CHUNK, GM_GROUPS)),
                  const((CONV_WIDTH, 2 * ML_W)), const((1, 2 * ML_W)), const((1, LANES)),
                  const((1, ML_W)), const((GM_W + ML_W, d))],
        out_specs=pl.BlockSpec((1, tt, d), lambda bi, ti: (bi, ti, 0)),
        out_shape=jax.ShapeDtypeStruct((b, s, d), F32),
        scratch_shapes=[pltpu.VMEM((CHUNK + 8, 2 * ML_W), F32),
                        pltpu.VMEM((ML_HEADS, CHUNK, 2 * LANES), F32),
                        pltpu.VMEM((ML_HEADS, 1, LANES), F32),
                        pltpu.VMEM((tt, GM_W + ML_W), BF16)],
        compiler_params=_params("arbitrary", "arbitrary"),
        name="even_mixer",
    )(proj, x, vgain, ws, sbt, convw, convb, gbias, hgain, wout)


def _mla_prep_kernel(x_ref, g_ref, win_ref, qg_ref, kvg_ref, wuq_ref, wukv_ref, cs_ref,
                     q_ref, k_ref, v_ref):
    xn = _rms(x_ref[0], g_ref[...])
    proj = _dot(xn.astype(BF16), win_ref[...])
    c_q = _rms(proj[:, :Q_LORA], qg_ref[...])
    c_kv = _rms(proj[:, Q_LORA:Q_LORA + KV_LORA], kvg_ref[...])
    cs = cs_ref[...]
    tm = cs.shape[0]
    lane = lax.broadcasted_iota(jnp.int32, (tm, LANES), 1)
    t = proj[:, Q_LORA + KV_LORA:] * cs
    k_rope = jnp.where(lane < QK_ROPE, t + pltpu.roll(t, QK_ROPE, axis=1), 0.0).astype(BF16)
    q_all = _dot(c_q.astype(BF16), wuq_ref[...])
    kv_all = _dot(c_kv.astype(BF16), wukv_ref[...])
    scale = float(QK_NOPE + QK_ROPE) ** -0.5
    ones_col = jnp.where(lane == 0, 1.0, 0.0).astype(BF16)
    for h in range(MLA_HEADS):
        o = h * QK_PAD
        qt = q_all[:, o + QK_NOPE:o + QK_PAD] * cs
        q_ref[0, h, :, 0:QK_NOPE] = (q_all[:, o:o + QK_NOPE] * scale).astype(BF16)
        q_ref[0, h, :, QK_NOPE:QK_PAD] = ((qt + pltpu.roll(qt, QK_ROPE, axis=1)) * scale).astype(BF16)
        k_ref[0, h, :, 0:QK_NOPE] = kv_all[:, o:o + QK_NOPE].astype(BF16)
        k_ref[0, h, :, QK_NOPE:QK_PAD] = k_rope
        v_ref[0, h, :, 0:V_DIM] = kv_all[:, o + QK_NOPE:o + QK_PAD].astype(BF16)
        v_ref[0, h, :, V_DIM:V_PAD] = ones_col


def mla_prep(x, gain, win, qg, kvg, wuq, wukv, cs, tm=512):
    b, s, d = x.shape
    tm = min(tm, s)
    const = lambda shape: pl.BlockSpec(shape, lambda bi, ti: (0,) * len(shape))
    head_out = lambda w: pl.BlockSpec((1, MLA_HEADS, tm, w), lambda bi, ti: (bi, 0, ti, 0))
    return pl.pallas_call(
        _mla_prep_kernel,
        grid=(b, s // tm),
        in_specs=[pl.BlockSpec((1, tm, d), lambda bi, ti: (bi, ti, 0)),
                  const((1, d)), const(win.shape), const((1, Q_LORA)), const((1, KV_LORA)),
                  const(wuq.shape), const(wukv.shape),
                  pl.BlockSpec((tm, LANES), lambda bi, ti: (ti, 0))],
        out_specs=[head_out(QK_PAD), head_out(QK_PAD), head_out(V_PAD)],
        out_shape=[jax.ShapeDtypeStruct((b, MLA_HEADS, s, QK_PAD), BF16),
                   jax.ShapeDtypeStruct((b, MLA_HEADS, s, QK_PAD), BF16),
                   jax.ShapeDtypeStruct((b, MLA_HEADS, s, V_PAD), BF16)],
        compiler_params=_params("parallel", "parallel"),
        name="mla_prep",
    )(x, gain, win, qg, kvg, wuq, wukv, cs)


def _flash_kernel(qi_ref, ki_ref, q_ref, k_ref, v_ref, o_ref, m_sc, acc_sc, *, blk):
    step = pl.program_id(1)
    qi = qi_ref[step]
    ki = ki_ref[step]

    @pl.when(ki == 0)
    def _():
        m_sc[...] = jnp.full(m_sc.shape, -jnp.inf, F32)
        acc_sc[...] = jnp.zeros(acc_sc.shape, F32)

    def sweep(diagonal):
        def scores(h):
            return _dot_nt(q_ref[0, h], k_ref[0, h])

        s_next = scores(0)
        for h in range(MLA_HEADS):
            s = s_next
            if h + 1 < MLA_HEADS:
                s_next = scores(h + 1)
            if diagonal:
                r = lax.broadcasted_iota(jnp.int32, (blk, blk), 0)
                c = lax.broadcasted_iota(jnp.int32, (blk, blk), 1)
                s = jnp.where(c <= r, s, -jnp.inf)
            m_prev = m_sc[h]
            m_new = jnp.maximum(m_prev, jnp.max(s, axis=1, keepdims=True))
            alpha = jnp.exp(m_prev - m_new)
            p = jnp.exp((s - m_new[:, 0:1]).astype(BF16))
            acc_sc[h] = jnp.concatenate([alpha, alpha], axis=1) * acc_sc[h] + _dot(p, v_ref[0, h])
            m_sc[h] = m_new

    @pl.when(ki < qi)
    def _():
        sweep(False)

    @pl.when(ki == qi)
    def _():
        sweep(True)
        for h in range(MLA_HEADS):
            acc = acc_sc[h]
            o_ref[0, :, h * V_DIM:(h + 1) * V_DIM] = (acc[:, :V_DIM] / acc[:, V_DIM:V_DIM + 1]).astype(BF16)


def flash_attention(q, k, v, blk=512):
    b, nh, s, _ = q.shape
    blk = min(blk, s)
    nq = s // blk
    qi = np.concatenate([np.full(i + 1, i, np.int32) for i in range(nq)])
    ki = np.concatenate([np.arange(i + 1, dtype=np.int32) for i in range(nq)])
    grid_spec = pltpu.PrefetchScalarGridSpec(
        num_scalar_prefetch=2,
        grid=(b, len(qi)),
        in_specs=[pl.BlockSpec((1, nh, blk, QK_PAD), lambda bi, p, qi_r, ki_r: (bi, 0, qi_r[p], 0)),
                  pl.BlockSpec((1, nh, blk, QK_PAD), lambda bi, p, qi_r, ki_r: (bi, 0, ki_r[p], 0)),
                  pl.BlockSpec((1, nh, blk, V_PAD), lambda bi, p, qi_r, ki_r: (bi, 0, ki_r[p], 0))],
        out_specs=pl.BlockSpec((1, blk, nh * V_DIM), lambda bi, p, qi_r, ki_r: (bi, qi_r[p], 0)),
        scratch_shapes=[pltpu.VMEM((nh, blk, LANES), F32),
                        pltpu.VMEM((nh, blk, V_PAD), F32)],
    )
    return pl.pallas_call(
        functools.partial(_flash_kernel, blk=blk),
        grid_spec=grid_spec,
        out_shape=jax.ShapeDtypeStruct((b, s, nh * V_DIM), BF16),
        compiler_params=_params("arbitrary", "arbitrary"),
        name="mla_attention",
    )(jnp.asarray(qi), jnp.asarray(ki), q, k, v)


def _top16(e):
    sub = lax.broadcasted_iota(jnp.int32, (PEER_TOPK, e.shape[1]), 0)
    top = jnp.zeros((PEER_TOPK, e.shape[1]), F32)
    cur = e
    for r in range(PEER_TOPK):
        mk = jnp.max(cur, axis=0, keepdims=True)
        top = jnp.where(sub == r, mk, top)
        if r + 1 < PEER_TOPK:
            cur = jnp.where(cur == mk, -1.0, cur)
    return jnp.maximum(top, 0.0)


def _pair_candidates(ea, eb):
    parts = [ea[0:1] * eb]
    parts += [ea[p:p + 1] * eb[0:8] for p in range(1, 8)]
    parts += [ea[8:16] * eb[0:1]]
    return jnp.concatenate(parts, axis=0)


def _router_kernel(x_ref, g_ref, wq_ref, keys_ref, xn_ref, e1_ref, e2_ref, tau_ref):
    xn = _rms(x_ref[...], g_ref[...]).astype(BF16)
    xn_ref[...] = xn
    q = _dot(xn, wq_ref[...]).astype(BF16)
    for h in range(PEER_HEADS):
        es, tops = [], []
        for half in range(2):
            j = 2 * h + half
            s_t = _dot_nt(keys_ref[j], q[:, j * PEER_HALF:(j + 1) * PEER_HALF])
            e = jnp.exp(s_t - jnp.max(s_t, axis=0, keepdims=True))
            es.append(e)
            tops.append(_top16(e))
        ea, eb = tops
        cand = _pair_candidates(ea, eb)
        cur = cand
        for r in range(PEER_TOPK):
            kth = jnp.max(cur, axis=0, keepdims=True)
            if r + 1 < PEER_TOPK:
                cur = jnp.where(cur == kth, -1.0, cur)
        sel = cand >= jnp.maximum(kth, 1e-30)
        z = jnp.sum(jnp.where(sel, cand, 0.0), axis=0, keepdims=True)
        inv_z = 1.0 / z
        cand_z = _pair_candidates(ea * inv_z, eb)
        tau_ref[h:h + 1, :] = jnp.min(jnp.where(sel, cand_z, jnp.inf), axis=0, keepdims=True)
        e1_ref[h] = es[0] * inv_z
        e2_ref[h] = es[1]


def peer_router(x, gain, wq, keys, tr=256):
    n, d = x.shape
    tr = min(tr, n)
    return pl.pallas_call(
        _router_kernel,
        grid=(n // tr,),
        in_specs=[pl.BlockSpec((tr, d), lambda i: (i, 0)),
                  pl.BlockSpec((1, d), lambda i: (0, 0)),
                  pl.BlockSpec(wq.shape, lambda i: (0, 0)),
                  pl.BlockSpec(keys.shape, lambda i: (0, 0, 0))],
        out_specs=[pl.BlockSpec((tr, d), lambda i: (i, 0)),
                   pl.BlockSpec((PEER_HEADS, N_KEYS, tr), lambda i: (0, 0, i)),
                   pl.BlockSpec((PEER_HEADS, N_KEYS, tr), lambda i: (0, 0, i)),
                   pl.BlockSpec((PEER_HEADS, tr), lambda i: (0, i))],
        out_shape=[jax.ShapeDtypeStruct((n, d), BF16),
                   jax.ShapeDtypeStruct((PEER_HEADS, N_KEYS, n), F32),
                   jax.ShapeDtypeStruct((PEER_HEADS, N_KEYS, n), F32),
                   jax.ShapeDtypeStruct((PEER_HEADS, n), F32)],
        compiler_params=_params("parallel"),
        name="peer_router",
    )(x, gain.reshape(1, d), wq, keys)


def _peer_expert_kernel(xn_ref, x_ref, e1_ref, e2_ref, tau_ref, u_ref, v_ref, o_ref,
                        acc_sc, act_sc, w_sc, *, et, n_sub, tc):
    e = pl.program_id(1)
    tb = xn_ref.shape[0]
    n_i = et // N_KEYS

    @pl.when(e == 0)
    def _():
        acc_sc[...] = jnp.zeros(acc_sc.shape, F32)

    mxu_w = 2 * LANES

    def rows(s):
        return slice(s * et, (s + 1) * et)

    def act_pieces(s):
        def piece(c):
            cols = slice(c * mxu_w, (c + 1) * mxu_w)
            act_sc[s, :, cols] = _dot_nt(u_ref[rows(s), :], xn_ref[cols, :])
        return [functools.partial(piece, c) for c in range(tb // mxu_w)]

    def out_pieces(s):
        def piece(c):
            cols = slice(c * mxu_w, (c + 1) * mxu_w)
            acc_sc[:, cols] += _dot_tn(w_sc[s], v_ref[rows(s), cols])
        return [functools.partial(piece, c) for c in range(acc_sc.shape[1] // mxu_w)]

    def build_pieces(s):
        def piece(i, c):
            ig = (e * n_sub + s) * n_i + i
            r = slice(i * N_KEYS, (i + 1) * N_KEYS)
            cols = slice(c * tc, (c + 1) * tc)
            gates = jnp.zeros((N_KEYS, tc), F32)
            for h in range(PEER_HEADS):
                p = e2_ref[h, :, cols] * e1_ref[h, pl.ds(ig, 1), :][:, cols]
                gates = gates + jnp.where(p >= tau_ref[h:h + 1, cols], p, 0.0)
            w_sc[s, r, cols] = (gates * _gelu(act_sc[s, r, cols])).astype(BF16)
        return [functools.partial(piece, i, c) for i in range(n_i) for c in range(tb // tc)]

    def interleave(vector_work, matrix_work):
        done = 0
        for k, piece in enumerate(vector_work):
            piece()
            due = (k + 1) * len(matrix_work) // len(vector_work)
            for m in matrix_work[done:due]:
                m()
            done = due

    for m in act_pieces(0):
        m()
    for s in range(n_sub):
        matrix_work = (act_pieces(s + 1) if s + 1 < n_sub else []) + (out_pieces(s - 1) if s > 0 else [])
        interleave(build_pieces(s), matrix_work)
    for m in out_pieces(n_sub - 1):
        m()

    @pl.when(e == pl.num_programs(1) - 1)
    def _():
        o_ref[...] = x_ref[...] + acc_sc[...]


def peer_experts(xn, x, e1, e2, tau, u, v, tb=512, et=512, n_sub=4, tc=128):
    n, d = x.shape
    ne = u.shape[0]
    tb = min(tb, n)
    tc = min(tc, tb)
    blk_e = et * n_sub
    return pl.pallas_call(
        functools.partial(_peer_expert_kernel, et=et, n_sub=n_sub, tc=tc),
        grid=(n // tb, ne // blk_e),
        in_specs=[pl.BlockSpec((tb, d), lambda i, j: (i, 0)),
                  pl.BlockSpec((tb, d), lambda i, j: (i, 0)),
                  pl.BlockSpec((PEER_HEADS, N_KEYS, tb), lambda i, j: (0, 0, i)),
                  pl.BlockSpec((PEER_HEADS, N_KEYS, tb), lambda i, j: (0, 0, i)),
                  pl.BlockSpec((PEER_HEADS, tb), lambda i, j: (0, i)),
                  pl.BlockSpec((blk_e, d), lambda i, j: (j, 0)),
                  pl.BlockSpec((blk_e, d), lambda i, j: (j, 0))],
        out_specs=pl.BlockSpec((tb, d), lambda i, j: (i, 0)),
        out_shape=jax.ShapeDtypeStruct((n, d), F32),
        scratch_shapes=[pltpu.VMEM((tb, d), F32),
                        pltpu.VMEM((n_sub, et, tb), F32),
                        pltpu.VMEM((n_sub, et, tb), BF16)],
        compiler_params=_params("parallel", "arbitrary"),
        name="peer_experts",
    )(xn, x, e1, e2, tau, u, v)


def _final_norm_kernel(x_ref, g_ref, o_ref):
    o_ref[...] = _rms(x_ref[...], g_ref[...])


def final_norm(x, gain, tm=1024):
    n, d = x.shape
    tm = min(tm, n)
    return pl.pallas_call(
        _final_norm_kernel,
        grid=(n // tm,),
        in_specs=[pl.BlockSpec((tm, d), lambda i: (i, 0)), pl.BlockSpec((1, d), lambda i: (0, 0))],
        out_specs=pl.BlockSpec((tm, d), lambda i: (i, 0)),
        out_shape=jax.ShapeDtypeStruct((n, d), F32),
        compiler_params=_params("parallel"),
        name="final_norm",
    )(x, gain.reshape(1, d))


def _swap_halves(w):
    half = w.shape[-1] // 2
    return jnp.concatenate([w[..., half:], w[..., :half]], axis=-1)


def _even_layer(x, gain, w_in, v_gain, ws, sb, conv_w, conv_b, i_bias, f_bias, h_gain, w_out):
    b, s, d = x.shape
    w_in_p = jnp.pad(w_in, ((0, 0), (0, EV_IN_PAD - EV_IN))).astype(BF16)
    proj = norm_matmul(x.reshape(b * s, d), gain, w_in_p).reshape(b, s, EV_IN_PAD)
    gbias = jnp.pad(jnp.concatenate([i_bias, f_bias]), (0, LANES - 2 * ML_HEADS)).reshape(1, LANES)
    return even_mixer(proj, x, v_gain.reshape(1, GM_W), ws, sb.T, conv_w, conv_b.reshape(1, 2 * ML_W),
                      gbias, h_gain.reshape(1, ML_W), w_out.astype(BF16))


def _odd_layer(x, gain, w_in, q_gain, kv_gain, w_uq, w_ukv, w_out, cs):
    b, s, d = x.shape
    rope_cols = w_in[:, Q_LORA + KV_LORA:]
    win = jnp.concatenate([w_in, _swap_halves(rope_cols)], axis=1).astype(BF16)
    wq = w_uq.reshape(Q_LORA, MLA_HEADS, QK_NOPE + QK_ROPE)
    wuq = jnp.concatenate([wq, _swap_halves(wq[..., QK_NOPE:])], axis=-1)
    wuq = wuq.reshape(Q_LORA, MLA_HEADS * QK_PAD).astype(BF16)
    q, k, v = mla_prep(x, gain.reshape(1, d), win, q_gain.reshape(1, Q_LORA), kv_gain.reshape(1, KV_LORA),
                       wuq, w_ukv.astype(BF16), cs)
    o = flash_attention(q, k, v)
    return matmul_residual(o.reshape(b * s, MLA_HEADS * V_DIM), w_out.astype(BF16),
                           x.reshape(b * s, d)).reshape(b, s, d)


def _peer_layer(x, gain, w_q, keys, u_tab, v_tab):
    b, s, d = x.shape
    x2 = x.reshape(b * s, d)
    keys16 = keys.reshape(PEER_HEADS * 2, N_KEYS, PEER_HALF).astype(BF16)
    xn, e1, e2, tau = peer_router(x2, gain, w_q.astype(BF16), keys16)
    out = peer_experts(xn, x2, e1, e2, tau, u_tab.astype(BF16), v_tab.astype(BF16))
    return out.reshape(b, s, d)


def kernel(x, norm_mix, norm_ffn, norm_final, ev_w_in, ev_gm_v_gain, ev_gm_ws, ev_gm_b, ev_conv_w, ev_conv_b, ev_i_bias, ev_f_bias, ev_h_gain, ev_w_out, od_w_in, od_q_gain, od_kv_gain, od_w_uq, od_w_ukv, od_w_out, peer_w_q, peer_keys, peer_u, peer_v):
    b, s, d = x.shape
    depth = norm_mix.shape[0]
    pos = jnp.arange(s, dtype=F32)
    inv_freq = ROPE_THETA ** (-jnp.arange(QK_ROPE // 2, dtype=F32) / (QK_ROPE // 2))
    ang = pos[:, None] * inv_freq[None, :]
    cos, sin = jnp.cos(ang), jnp.sin(ang)
    cs = jnp.concatenate([cos, cos, -sin, sin], axis=1)
    for layer in range(depth):
        j = layer // 2
        if layer % 2 == 0:
            x = _even_layer(x, norm_mix[layer], ev_w_in[j], ev_gm_v_gain[j], ev_gm_ws[j], ev_gm_b[j],
                            ev_conv_w[j], ev_conv_b[j], ev_i_bias[j], ev_f_bias[j], ev_h_gain[j], ev_w_out[j])
        else:
            x = _odd_layer(x, norm_mix[layer], od_w_in[j], od_q_gain[j], od_kv_gain[j], od_w_uq[j],
                           od_w_ukv[j], od_w_out[j], cs)
        x = _peer_layer(x, norm_ffn[layer], peer_w_q[layer], peer_keys[layer], peer_u[layer], peer_v[layer])
    return final_norm(x.reshape(b * s, d), norm_final).reshape(b, s, d)
```

```python
import functools
import math

import jax
import jax.numpy as jnp
import numpy as np
from jax import lax
from jax.experimental import pallas as pl
from jax.experimental.pallas import tpu as pltpu

F32 = jnp.float32
BF16 = jnp.bfloat16
EPS = 1e-6
LANES = 128
GELU_C1 = math.sqrt(2.0 / math.pi)
GELU_C3 = 0.044715 * GELU_C1
VMEM_LIMIT = 56 * 1024 * 1024

GM_GROUPS = 4
GM_W = 512
ML_HEADS = 4
ML_W = 512
CHUNK = 128
CONV_WIDTH = 4
EV_IN = 2 * GM_W + 4 * ML_W + 2 * ML_HEADS
EV_IN_PAD = 2 * GM_W + 4 * ML_W + LANES
MLA_HEADS = 8
QK_NOPE = 128
QK_ROPE = 64
V_DIM = 128
Q_LORA = 512
KV_LORA = 256
ROPE_THETA = 10000.0
QK_PAD = 256
V_PAD = 256
PEER_HEADS = 8
N_KEYS = 128
PEER_TOPK = 16
PEER_HALF = 128


def _params(*sem):
    return pltpu.CompilerParams(dimension_semantics=sem, vmem_limit_bytes=VMEM_LIMIT)


def _rms(x, gain):
    return x * lax.rsqrt(jnp.mean(x * x, axis=-1, keepdims=True) + EPS) * gain


def _gelu(x):
    c = math.sqrt(2.0 / math.pi)
    return x * (0.5 * (1.0 + jnp.tanh(c * (x + 0.044715 * (x * x * x)))))


def _sigmoid(x):
    return 1.0 / (1.0 + jnp.exp(-x))


def _dot(a, b):
    return jnp.dot(a, b, preferred_element_type=F32)


def _dot_nt(a, b):
    return lax.dot_general(a, b, (((1,), (1,)), ((), ())), preferred_element_type=F32)


def _dot_tn(a, b):
    return lax.dot_general(a, b, (((0,), (0,)), ((), ())), preferred_element_type=F32)


def _norm_matmul_kernel(x_ref, g_ref, w_ref, o_ref):
    xn = _rms(x_ref[...], g_ref[...])
    o_ref[...] = _dot(xn.astype(BF16), w_ref[...])


def norm_matmul(x, gain, w, tm=512):
    n, d = x.shape
    m = w.shape[1]
    tm = min(tm, n)
    return pl.pallas_call(
        _norm_matmul_kernel,
        grid=(n // tm,),
        in_specs=[pl.BlockSpec((tm, d), lambda i: (i, 0)),
                  pl.BlockSpec((1, d), lambda i: (0, 0)),
                  pl.BlockSpec((d, m), lambda i: (0, 0))],
        out_specs=pl.BlockSpec((tm, m), lambda i: (i, 0)),
        out_shape=jax.ShapeDtypeStruct((n, m), F32),
        compiler_params=_params("parallel"),
        name="norm_matmul",
    )(x, gain.reshape(1, d), w)


def _matmul_residual_kernel(a_ref, w_ref, x_ref, o_ref):
    o_ref[...] = x_ref[...] + _dot(a_ref[...], w_ref[...])


def matmul_residual(a, w, x, tm=512):
    n, k = a.shape
    d = w.shape[1]
    tm = min(tm, n)
    return pl.pallas_call(
        _matmul_residual_kernel,
        grid=(n // tm,),
        in_specs=[pl.BlockSpec((tm, k), lambda i: (i, 0)),
                  pl.BlockSpec((k, d), lambda i: (0, 0)),
                  pl.BlockSpec((tm, d), lambda i: (i, 0))],
        out_specs=pl.BlockSpec((tm, d), lambda i: (i, 0)),
        out_shape=jax.ShapeDtypeStruct((n, d), F32),
        compiler_params=_params("parallel"),
        name="matmul_residual",
    )(a, w, x)


def _even_mixer_kernel(proj_ref, x_ref, vgain_ref, ws_ref, sbt_ref, convw_ref, convb_ref,
                       gbias_ref, hgain_ref, wout_ref, o_ref,
                       zbuf, cext, mstate, mixed, *, n_chunks):
    @pl.when(pl.program_id(1) == 0)
    def _():
        zbuf[0:8, :] = jnp.zeros((8, 2 * ML_W), F32)
        cext[...] = jnp.zeros(cext.shape, F32)
        mstate[...] = jnp.zeros(mstate.shape, F32)

    row = lax.broadcasted_iota(jnp.int32, (CHUNK, CHUNK), 0)
    col = lax.broadcasted_iota(jnp.int32, (CHUNK, CHUNK), 1)
    causal = col <= row
    tri = jnp.where(causal, 1.0, 0.0).astype(F32)
    ones_col = jnp.where(col == 0, 1.0, 0.0).astype(F32)
    q_scale = float(CHUNK) ** -0.5

    for c in range(n_chunks):
        rows = slice(c * CHUNK, (c + 1) * CHUNK)

        for g in range(GM_GROUPS):
            lanes = slice(g * LANES, (g + 1) * LANES)
            u = _gelu(proj_ref[0, rows, lanes])
            v = _gelu(proj_ref[0, rows, GM_W + g * LANES:GM_W + (g + 1) * LANES])
            vn = _rms(v, vgain_ref[:, lanes])
            wm = jnp.where(causal, ws_ref[g], 0.0)
            s = _dot(wm.astype(BF16), vn.astype(BF16)) + sbt_ref[:, g:g + 1]
            mixed[rows, lanes] = (u * s).astype(BF16)

        zbuf[8:8 + CHUNK, :] = proj_ref[0, rows, 2 * GM_W:2 * GM_W + 2 * ML_W]
        conv = convb_ref[...]
        for j in range(CONV_WIDTH):
            lo = 8 - (CONV_WIDTH - 1) + j
            conv = conv + convw_ref[j:j + 1, :] * zbuf[lo:lo + CHUNK, :]
        zbuf[0:8, :] = zbuf[CHUNK:CHUNK + 8, :]
        qk = conv * _sigmoid(conv)

        gt = proj_ref[0, rows, EV_IN_PAD - LANES:EV_IN_PAD] + gbias_ref[...]
        logf = jnp.minimum(gt, 0.0) - jnp.log1p(jnp.exp(-jnp.abs(gt)))
        bcum = jnp.dot(tri, logf, preferred_element_type=F32, precision=lax.Precision.HIGHEST)
        gt_t = gt.T
        bcum_t = bcum.T

        for hd in range(ML_HEADS):
            lanes = slice(hd * LANES, (hd + 1) * LANES)
            q = (qk[:, lanes] * q_scale).astype(BF16)
            kf = qk[:, ML_W + hd * LANES:ML_W + (hd + 1) * LANES]
            k = kf.astype(BF16)
            v = proj_ref[0, rows, 2 * GM_W + 2 * ML_W + hd * LANES:2 * GM_W + 2 * ML_W + (hd + 1) * LANES]
            o_pre = proj_ref[0, rows, 2 * GM_W + 3 * ML_W + hd * LANES:2 * GM_W + 3 * ML_W + (hd + 1) * LANES]
            b_col = bcum[:, ML_HEADS + hd:ML_HEADS + hd + 1]
            b_row = bcum_t[ML_HEADS + hd:ML_HEADS + hd + 1, :]
            i_col = gt[:, hd:hd + 1]
            i_row = gt_t[hd:hd + 1, :]
            m_prev = mstate[hd][:, 0:1]
            c_prev = cext[hd]

            dmat = jnp.where(causal, b_col - b_row + i_row, -jnp.inf)
            inter = b_col + m_prev
            m_t = jnp.maximum(inter, jnp.max(dmat, axis=-1, keepdims=True))
            w_intra = jnp.exp(dmat - m_t)
            w_inter = jnp.exp(inter - m_t)
            sm = _dot_nt(q, k) * w_intra
            v_ext = jnp.concatenate([v, ones_col], axis=1)
            numden = _dot(sm.astype(BF16), v_ext.astype(BF16)) + w_inter * _dot(q, c_prev.astype(BF16))
            num = numden[:, :LANES]
            den = numden[:, LANES:LANES + 1]
            h = num / jnp.maximum(jnp.abs(den), jnp.exp(-m_t))

            b_last = bcum[CHUNK - 1:CHUNK, ML_HEADS + hd:ML_HEADS + hd + 1]
            gdec = b_last - b_col + i_col
            m_new = jnp.maximum(b_last + m_prev, jnp.max(gdec, axis=0, keepdims=True))
            decay = jnp.exp(b_last + m_prev - m_new)
            w_new = jnp.exp(gdec - m_new)
            cext[hd] = decay * c_prev + _dot(kf.T.astype(BF16), (w_new * v_ext).astype(BF16))
            mstate[hd] = jnp.broadcast_to(m_new, (1, LANES))

            hn = _rms(h, hgain_ref[:, lanes])
            mixed[rows, GM_W + hd * LANES:GM_W + (hd + 1) * LANES] = (_sigmoid(o_pre) * hn).astype(BF16)

    o_ref[0] = x_ref[0] + _dot(mixed[...], wout_ref[...])


def even_mixer(proj, x, vgain, ws, sbt, convw, convb, gbias, hgain, wout, tt=256):
    b, s, d = x.shape
    tt = min(tt, s)
    const = lambda shape: pl.BlockSpec(shape, lambda bi, ti: (0,) * len(shape))
    return pl.pallas_call(
        functools.partial(_even_mixer_kernel, n_chunks=tt // CHUNK),
        grid=(b, s // tt),
        in_specs=[pl.BlockSpec((1, tt, EV_IN_PAD), lambda bi, ti: (bi, ti, 0)),
                  pl.BlockSpec((1, tt, d), lambda bi, ti: (bi, ti, 0)),
                  const((1, GM_W)), const((GM_GROUPS, CHUNK, CHUNK)), const((CHUNK, GM_GROUPS)),
                  const((CONV_WIDTH, 2 * ML_W)), const((1, 2 * ML_W)), const((1, LANES)),
                  const((1, ML_W)), const((GM_W + ML_W, d))],
        out_specs=pl.BlockSpec((1, tt, d), lambda bi, ti: (bi, ti, 0)),
        out_shape=jax.ShapeDtypeStruct((b, s, d), F32),
        scratch_shapes=[pltpu.VMEM((CHUNK + 8, 2 * ML_W), F32),
                        pltpu.VMEM((ML_HEADS, CHUNK, 2 * LANES), F32),
                        pltpu.VMEM((ML_HEADS, 1, LANES), F32),
                        pltpu.VMEM((tt, GM_W + ML_W), BF16)],
        compiler_params=_params("arbitrary", "arbitrary"),
        name="even_mixer",
    )(proj, x, vgain, ws, sbt, convw, convb, gbias, hgain, wout)


def _mla_prep_kernel(x_ref, g_ref, win_ref, qg_ref, kvg_ref, wuq_ref, wukv_ref, cs_ref,
                     q_ref, k_ref, v_ref):
    xn = _rms(x_ref[0], g_ref[...])
    proj = _dot(xn.astype(BF16), win_ref[...])
    c_q = _rms(proj[:, :Q_LORA], qg_ref[...])
    c_kv = _rms(proj[:, Q_LORA:Q_LORA + KV_LORA], kvg_ref[...])
    cs = cs_ref[...]
    tm = cs.shape[0]
    lane = lax.broadcasted_iota(jnp.int32, (tm, LANES), 1)
    t = proj[:, Q_LORA + KV_LORA:] * cs
    k_rope = jnp.where(lane < QK_ROPE, t + pltpu.roll(t, QK_ROPE, axis=1), 0.0).astype(BF16)
    q_all = _dot(c_q.astype(BF16), wuq_ref[...])
    kv_all = _dot(c_kv.astype(BF16), wukv_ref[...])
    scale = float(QK_NOPE + QK_ROPE) ** -0.5
    ones_col = jnp.where(lane == 0, 1.0, 0.0).astype(BF16)
    for h in range(MLA_HEADS):
        o = h * QK_PAD
        qt = q_all[:, o + QK_NOPE:o + QK_PAD] * cs
        q_ref[0, h, :, 0:QK_NOPE] = (q_all[:, o:o + QK_NOPE] * scale).astype(BF16)
        q_ref[0, h, :, QK_NOPE:QK_PAD] = ((qt + pltpu.roll(qt, QK_ROPE, axis=1)) * scale).astype(BF16)
        k_ref[0, h, :, 0:QK_NOPE] = kv_all[:, o:o + QK_NOPE].astype(BF16)
        k_ref[0, h, :, QK_NOPE:QK_PAD] = k_rope
        v_ref[0, h, :, 0:V_DIM] = kv_all[:, o + QK_NOPE:o + QK_PAD].astype(BF16)
        v_ref[0, h, :, V_DIM:V_PAD] = ones_col


def mla_prep(x, gain, win, qg, kvg, wuq, wukv, cs, tm=512):
    b, s, d = x.shape
    tm = min(tm, s)
    const = lambda shape: pl.BlockSpec(shape, lambda bi, ti: (0,) * len(shape))
    head_out = lambda w: pl.BlockSpec((1, MLA_HEADS, tm, w), lambda bi, ti: (bi, 0, ti, 0))
    return pl.pallas_call(
        _mla_prep_kernel,
        grid=(b, s // tm),
        in_specs=[pl.BlockSpec((1, tm, d), lambda bi, ti: (bi, ti, 0)),
                  const((1, d)), const(win.shape), const((1, Q_LORA)), const((1, KV_LORA)),
                  const(wuq.shape), const(wukv.shape),
                  pl.BlockSpec((tm, LANES), lambda bi, ti: (ti, 0))],
        out_specs=[head_out(QK_PAD), head_out(QK_PAD), head_out(V_PAD)],
        out_shape=[jax.ShapeDtypeStruct((b, MLA_HEADS, s, QK_PAD), BF16),
                   jax.ShapeDtypeStruct((b, MLA_HEADS, s, QK_PAD), BF16),
                   jax.ShapeDtypeStruct((b, MLA_HEADS, s, V_PAD), BF16)],
        compiler_params=_params("parallel", "parallel"),
        name="mla_prep",
    )(x, gain, win, qg, kvg, wuq, wukv, cs)


def _flash_kernel(qi_ref, ki_ref, q_ref, k_ref, v_ref, o_ref, m_sc, acc_sc, *, blk):
    step = pl.program_id(1)
    qi = qi_ref[step]
    ki = ki_ref[step]

    @pl.when(ki == 0)
    def _():
        m_sc[...] = jnp.full(m_sc.shape, -jnp.inf, F32)
        acc_sc[...] = jnp.zeros(acc_sc.shape, F32)

    def sweep(diagonal):
        def scores(h):
            return _dot_nt(q_ref[0, h], k_ref[0, h])

        s_next = scores(0)
        for h in range(MLA_HEADS):
            s = s_next
            if h + 1 < MLA_HEADS:
                s_next = scores(h + 1)
            if diagonal:
                r = lax.broadcasted_iota(jnp.int32, (blk, blk), 0)
                c = lax.broadcasted_iota(jnp.int32, (blk, blk), 1)
                s = jnp.where(c <= r, s, -jnp.inf)
            m_prev = m_sc[h]
            m_new = jnp.maximum(m_prev, jnp.max(s, axis=1, keepdims=True))
            alpha = jnp.exp(m_prev - m_new)
            p = jnp.exp((s - m_new[:, 0:1]).astype(BF16))
            acc_sc[h] = jnp.concatenate([alpha, alpha], axis=1) * acc_sc[h] + _dot(p, v_ref[0, h])
            m_sc[h] = m_new

    @pl.when(ki < qi)
    def _():
        sweep(False)

    @pl.when(ki == qi)
    def _():
        sweep(True)
        for h in range(MLA_HEADS):
            acc = acc_sc[h]
            o_ref[0, :, h * V_DIM:(h + 1) * V_DIM] = (acc[:, :V_DIM] / acc[:, V_DIM:V_DIM + 1]).astype(BF16)


def flash_attention(q, k, v, blk=512):
    b, nh, s, _ = q.shape
    blk = min(blk, s)
    nq = s // blk
    qi = np.concatenate([np.full(i + 1, i, np.int32) for i in range(nq)])
    ki = np.concatenate([np.arange(i + 1, dtype=np.int32) for i in range(nq)])
    grid_spec = pltpu.PrefetchScalarGridSpec(
        num_scalar_prefetch=2,
        grid=(b, len(qi)),
        in_specs=[pl.BlockSpec((1, nh, blk, QK_PAD), lambda bi, p, qi_r, ki_r: (bi, 0, qi_r[p], 0)),
                  pl.BlockSpec((1, nh, blk, QK_PAD), lambda bi, p, qi_r, ki_r: (bi, 0, ki_r[p], 0)),
                  pl.BlockSpec((1, nh, blk, V_PAD), lambda bi, p, qi_r, ki_r: (bi, 0, ki_r[p], 0))],
        out_specs=pl.BlockSpec((1, blk, nh * V_DIM), lambda bi, p, qi_r, ki_r: (bi, qi_r[p], 0)),
        scratch_shapes=[pltpu.VMEM((nh, blk, LANES), F32),
                        pltpu.VMEM((nh, blk, V_PAD), F32)],
    )
    return pl.pallas_call(
        functools.partial(_flash_kernel, blk=blk),
        grid_spec=grid_spec,
        out_shape=jax.ShapeDtypeStruct((b, s, nh * V_DIM), BF16),
        compiler_params=_params("arbitrary", "arbitrary"),
        name="mla_attention",
    )(jnp.asarray(qi), jnp.asarray(ki), q, k, v)


def _sorting_network(n):
    pairs = []
    p = 1
    while p < n:
        k = p
        while k >= 1:
            for j in range(k % p, n - k, 2 * k):
                for i in range(min(k, n - j - k)):
                    if (i + j) // (2 * p) == (i + j + k) // (2 * p):
                        pairs.append((i + j, i + j + k))
            k //= 2
        p *= 2
    return pairs


def _largest16(tiles, only_last):
    n = PEER_TOPK
    v = list(tiles)

    def exchange(i, j):
        v[i], v[j] = jnp.maximum(v[i], v[j]), jnp.minimum(v[i], v[j])

    for i, j in _sorting_network(n):
        exchange(i, j)
    shift = v[0].shape[0] // 2
    while shift >= 1:
        other = [pltpu.roll(x, shift, axis=0) for x in v]
        v = [jnp.maximum(v[i], other[n - 1 - i]) for i in range(n)]
        if only_last and shift == 1:
            return functools.reduce(jnp.minimum, v)
        stride = n // 2
        while stride >= 1:
            for i in range(n):
                if (i // stride) % 2 == 0:
                    exchange(i, i + stride)
            stride //= 2
        shift //= 2
    return v


def _top16(e):
    sub_rows = e.shape[0] // PEER_TOPK
    v = _largest16([e[r * sub_rows:(r + 1) * sub_rows] for r in range(PEER_TOPK)], only_last=False)
    sub = lax.broadcasted_iota(jnp.int32, v[0].shape, 0)
    halves = []
    for base in range(0, PEER_TOPK, sub_rows):
        rows = v[base]
        for r in range(1, sub_rows):
            rows = jnp.where(sub == r, v[base + r], rows)
        halves.append(rows)
    return jnp.concatenate(halves, axis=0)


def _pair_candidates(ea, eb):
    parts = [ea[0:1] * eb]
    parts += [ea[p:p + 1] * eb[0:8] for p in range(1, 8)]
    parts += [ea[8:16] * eb[0:1]]
    return jnp.concatenate(parts, axis=0)


def _router_kernel(x_ref, g_ref, wq_ref, keys_ref, xn_ref, e1_ref, e2_ref, tau_ref):
    xn = _rms(x_ref[...], g_ref[...]).astype(BF16)
    xn_ref[...] = xn
    q = _dot(xn, wq_ref[...]).astype(BF16)
    for h in range(PEER_HEADS):
        es, tops = [], []
        for half in range(2):
            j = 2 * h + half
            s_t = _dot_nt(keys_ref[j], q[:, j * PEER_HALF:(j + 1) * PEER_HALF])
            e = jnp.exp(s_t - jnp.max(s_t, axis=0, keepdims=True))
            es.append(e)
            tops.append(_top16(e))
        ea, eb = tops
        cand = _pair_candidates(ea, eb)
        n_tiles = cand.shape[0] // 8
        tiles = [cand[8 * r:8 * (r + 1)] for r in range(n_tiles)]
        tiles += [jnp.zeros_like(tiles[0])] * (PEER_TOPK - n_tiles)
        kth = _largest16(tiles, only_last=True)[0:1]
        sel = cand >= jnp.maximum(kth, 1e-30)
        z = jnp.sum(jnp.where(sel, cand, 0.0), axis=0, keepdims=True)
        scale = 0.5 / z
        cand_z = _pair_candidates(ea * scale, eb)
        tau_ref[h:h + 1, :] = jnp.min(jnp.where(sel, cand_z, jnp.inf), axis=0, keepdims=True)
        e1_ref[h] = es[0] * scale
        e2_ref[h] = es[1]


def peer_router(x, gain, wq, keys, tr=256):
    n, d = x.shape
    tr = min(tr, n)
    return pl.pallas_call(
        _router_kernel,
        grid=(n // tr,),
        in_specs=[pl.BlockSpec((tr, d), lambda i: (i, 0)),
                  pl.BlockSpec((1, d), lambda i: (0, 0)),
                  pl.BlockSpec(wq.shape, lambda i: (0, 0)),
                  pl.BlockSpec(keys.shape, lambda i: (0, 0, 0))],
        out_specs=[pl.BlockSpec((tr, d), lambda i: (i, 0)),
                   pl.BlockSpec((PEER_HEADS, N_KEYS, tr), lambda i: (0, 0, i)),
                   pl.BlockSpec((PEER_HEADS, N_KEYS, tr), lambda i: (0, 0, i)),
                   pl.BlockSpec((PEER_HEADS, tr), lambda i: (0, i))],
        out_shape=[jax.ShapeDtypeStruct((n, d), BF16),
                   jax.ShapeDtypeStruct((PEER_HEADS, N_KEYS, n), F32),
                   jax.ShapeDtypeStruct((PEER_HEADS, N_KEYS, n), F32),
                   jax.ShapeDtypeStruct((PEER_HEADS, n), F32)],
        compiler_params=_params("parallel"),
        name="peer_router",
    )(x, gain.reshape(1, d), wq, keys)


def _peer_expert_kernel(xn_ref, x_ref, e1_ref, e2_ref, tau_ref, u_ref, v_ref, o_ref,
                        acc_sc, act_sc, w_sc, *, et, n_sub, tc):
    e = pl.program_id(1)
    tb = xn_ref.shape[0]
    n_i = et // N_KEYS

    @pl.when(e == 0)
    def _():
        acc_sc[...] = jnp.zeros(acc_sc.shape, F32)

    mxu_w = 2 * LANES

    def rows(s):
        return slice(s * et, (s + 1) * et)

    def act_pieces(s):
        def piece(c):
            cols = slice(c * mxu_w, (c + 1) * mxu_w)
            act_sc[s, :, cols] = _dot_nt(u_ref[rows(s), :], xn_ref[cols, :])
        return [functools.partial(piece, c) for c in range(tb // mxu_w)]

    def out_pieces(s):
        def piece(c):
            cols = slice(c * mxu_w, (c + 1) * mxu_w)
            acc_sc[:, cols] += _dot_tn(w_sc[s], v_ref[rows(s), cols])
        return [functools.partial(piece, c) for c in range(acc_sc.shape[1] // mxu_w)]

    def build_pieces(s):
        def piece(i, c):
            ig = (e * n_sub + s) * n_i + i
            r = slice(i * N_KEYS, (i + 1) * N_KEYS)
            cols = slice(c * tc, (c + 1) * tc)

            gates = jnp.zeros((N_KEYS, tc), F32)
            for h in range(PEER_HEADS):
                p = e2_ref[h, :, cols] * e1_ref[h, pl.ds(ig, 1), :][:, cols]
                gates = gates + jnp.where(p >= tau_ref[h:h + 1, cols], p, 0.0)
            a = act_sc[s, r, cols]
            two_gelu = a * (1.0 + jnp.tanh(a * (GELU_C1 + GELU_C3 * (a * a))))
            w_sc[s, r, cols] = (gates * two_gelu).astype(BF16)
        return [functools.partial(piece, i, c) for i in range(n_i) for c in range(tb // tc)]

    def interleave(vector_work, matrix_work):
        done = 0
        for k, piece in enumerate(vector_work):
            piece()
            due = (k + 1) * len(matrix_work) // len(vector_work)
            for m in matrix_work[done:due]:
                m()
            done = due

    for m in act_pieces(0):
        m()
    for s in range(n_sub):
        matrix_work = (act_pieces(s + 1) if s + 1 < n_sub else []) + (out_pieces(s - 1) if s > 0 else [])
        interleave(build_pieces(s), matrix_work)
    for m in out_pieces(n_sub - 1):
        m()

    @pl.when(e == pl.num_programs(1) - 1)
    def _():
        o_ref[...] = x_ref[...] + acc_sc[...]


def peer_experts(xn, x, e1, e2, tau, u, v, tb=512, et=512, n_sub=4, tc=128):
    n, d = x.shape
    ne = u.shape[0]
    tb = min(tb, n)
    tc = min(tc, tb)
    blk_e = et * n_sub
    per_key = pl.BlockSpec((PEER_HEADS, N_KEYS, tb), lambda i, j: (0, 0, i))
    return pl.pallas_call(
        functools.partial(_peer_expert_kernel, et=et, n_sub=n_sub, tc=tc),
        grid=(n // tb, ne // blk_e),
        in_specs=[pl.BlockSpec((tb, d), lambda i, j: (i, 0)),
                  pl.BlockSpec((tb, d), lambda i, j: (i, 0)),
                  per_key, per_key, pl.BlockSpec((PEER_HEADS, tb), lambda i, j: (0, i)),
                  pl.BlockSpec((blk_e, d), lambda i, j: (j, 0)),
                  pl.BlockSpec((blk_e, d), lambda i, j: (j, 0))],
        out_specs=pl.BlockSpec((tb, d), lambda i, j: (i, 0)),
        out_shape=jax.ShapeDtypeStruct((n, d), F32),
        scratch_shapes=[pltpu.VMEM((tb, d), F32),
                        pltpu.VMEM((n_sub, et, tb), F32),
                        pltpu.VMEM((n_sub, et, tb), BF16)],
        compiler_params=_params("parallel", "arbitrary"),
        name="peer_experts",
    )(xn, x, e1, e2, tau, u, v)


def _final_norm_kernel(x_ref, g_ref, o_ref):
    o_ref[...] = _rms(x_ref[...], g_ref[...])


def final_norm(x, gain, tm=1024):
    n, d = x.shape
    tm = min(tm, n)
    return pl.pallas_call(
        _final_norm_kernel,
        grid=(n // tm,),
        in_specs=[pl.BlockSpec((tm, d), lambda i: (i, 0)), pl.BlockSpec((1, d), lambda i: (0, 0))],
        out_specs=pl.BlockSpec((tm, d), lambda i: (i, 0)),
        out_shape=jax.ShapeDtypeStruct((n, d), F32),
        compiler_params=_params("parallel"),
        name="final_norm",
    )(x, gain.reshape(1, d))


def _swap_halves(w):
    half = w.shape[-1] // 2
    return jnp.concatenate([w[..., half:], w[..., :half]], axis=-1)


def _even_layer(x, gain, w_in, v_gain, ws, sb, conv_w, conv_b, i_bias, f_bias, h_gain, w_out):
    b, s, d = x.shape
    w_in_p = jnp.pad(w_in, ((0, 0), (0, EV_IN_PAD - EV_IN))).astype(BF16)
    proj = norm_matmul(x.reshape(b * s, d), gain, w_in_p).reshape(b, s, EV_IN_PAD)
    gbias = jnp.pad(jnp.concatenate([i_bias, f_bias]), (0, LANES - 2 * ML_HEADS)).reshape(1, LANES)
    return even_mixer(proj, x, v_gain.reshape(1, GM_W), ws, sb.T, conv_w, conv_b.reshape(1, 2 * ML_W),
                      gbias, h_gain.reshape(1, ML_W), w_out.astype(BF16))


def _odd_layer(x, gain, w_in, q_gain, kv_gain, w_uq, w_ukv, w_out, cs):
    b, s, d = x.shape
    rope_cols = w_in[:, Q_LORA + KV_LORA:]
    win = jnp.concatenate([w_in, _swap_halves(rope_cols)], axis=1).astype(BF16)
    wq = w_uq.reshape(Q_LORA, MLA_HEADS, QK_NOPE + QK_ROPE)
    wuq = jnp.concatenate([wq, _swap_halves(wq[..., QK_NOPE:])], axis=-1)
    wuq = wuq.reshape(Q_LORA, MLA_HEADS * QK_PAD).astype(BF16)
    q, k, v = mla_prep(x, gain.reshape(1, d), win, q_gain.reshape(1, Q_LORA), kv_gain.reshape(1, KV_LORA),
                       wuq, w_ukv.astype(BF16), cs)
    o = flash_attention(q, k, v)
    return matmul_residual(o.reshape(b * s, MLA_HEADS * V_DIM), w_out.astype(BF16),
                           x.reshape(b * s, d)).reshape(b, s, d)


def _peer_layer(x, gain, w_q, keys, u_tab, v_tab):
    b, s, d = x.shape
    x2 = x.reshape(b * s, d)
    keys16 = keys.reshape(PEER_HEADS * 2, N_KEYS, PEER_HALF).astype(BF16)
    xn, e1, e2, tau = peer_router(x2, gain, w_q.astype(BF16), keys16)
    out = peer_experts(xn, x2, e1, e2, tau, u_tab.astype(BF16), v_tab.astype(BF16))
    return out.reshape(b, s, d)


def kernel(x, norm_mix, norm_ffn, norm_final, ev_w_in, ev_gm_v_gain, ev_gm_ws, ev_gm_b, ev_conv_w, ev_conv_b, ev_i_bias, ev_f_bias, ev_h_gain, ev_w_out, od_w_in, od_q_gain, od_kv_gain, od_w_uq, od_w_ukv, od_w_out, peer_w_q, peer_keys, peer_u, peer_v):
    b, s, d = x.shape
    depth = norm_mix.shape[0]
    pos = jnp.arange(s, dtype=F32)
    inv_freq = ROPE_THETA ** (-jnp.arange(QK_ROPE // 2, dtype=F32) / (QK_ROPE // 2))
    ang = pos[:, None] * inv_freq[None, :]
    cos, sin = jnp.cos(ang), jnp.sin(ang)
    cs = jnp.concatenate([cos, cos, -sin, sin], axis=1)
    for layer in range(depth):
        j = layer // 2
        if layer % 2 == 0:
            x = _even_layer(x, norm_mix[layer], ev_w_in[j], ev_gm_v_gain[j], ev_gm_ws[j], ev_gm_b[j],
                            ev_conv_w[j], ev_conv_b[j], ev_i_bias[j], ev_f_bias[j], ev_h_gain[j], ev_w_out[j])
        else:
            x = _odd_layer(x, norm_mix[layer], od_w_in[j], od_q_gain[j], od_kv_gain[j], od_w_uq[j],
                           od_w_ukv[j], od_w_out[j], cs)
        x = _peer_layer(x, norm_ffn[layer], peer_w_q[layer], peer_keys[layer], peer_u[layer], peer_v[layer])
    return final_norm(x.reshape(b * s, d), norm_final).reshape(b, s, d)
```

```python
import functools
import math

import jax
import jax.numpy as jnp
import numpy as np
from jax import lax
from jax.experimental import pallas as pl
from jax.experimental.pallas import tpu as pltpu

F32 = jnp.float32
BF16 = jnp.bfloat16
EPS = 1e-6
LANES = 128
GELU_C1 = math.sqrt(2.0 / math.pi)
GELU_C3 = 0.044715 * GELU_C1
VMEM_LIMIT = 56 * 1024 * 1024

GM_GROUPS = 4
GM_W = 512
ML_HEADS = 4
ML_W = 512
CHUNK = 128
CONV_WIDTH = 4
EV_IN = 2 * GM_W + 4 * ML_W + 2 * ML_HEADS
EV_IN_PAD = 2 * GM_W + 4 * ML_W + LANES
MLA_HEADS = 8
QK_NOPE = 128
QK_ROPE = 64
V_DIM = 128
Q_LORA = 512
KV_LORA = 256
ROPE_THETA = 10000.0
QK_PAD = 256
V_PAD = 256
PEER_HEADS = 8
N_KEYS = 128
PEER_TOPK = 16
PEER_HALF = 128


def _params(*sem):
    return pltpu.CompilerParams(dimension_semantics=sem, vmem_limit_bytes=VMEM_LIMIT)


def _rms(x, gain):
    return x * lax.rsqrt(jnp.mean(x * x, axis=-1, keepdims=True) + EPS) * gain


def _gelu(x):
    c = math.sqrt(2.0 / math.pi)
    return x * (0.5 * (1.0 + jnp.tanh(c * (x + 0.044715 * (x * x * x)))))


def _sigmoid(x):
    return 1.0 / (1.0 + jnp.exp(-x))


def _dot(a, b):
    return jnp.dot(a, b, preferred_element_type=F32)


def _dot_nt(a, b):
    return lax.dot_general(a, b, (((1,), (1,)), ((), ())), preferred_element_type=F32)


def _dot_tn(a, b):
    return lax.dot_general(a, b, (((0,), (0,)), ((), ())), preferred_element_type=F32)


def _norm_matmul_kernel(x_ref, g_ref, w_ref, o_ref):
    xn = _rms(x_ref[...], g_ref[...])
    o_ref[...] = _dot(xn.astype(BF16), w_ref[...])


def norm_matmul(x, gain, w, tm=512):
    n, d = x.shape
    m = w.shape[1]
    tm = min(tm, n)
    return pl.pallas_call(
        _norm_matmul_kernel,
        grid=(n // tm,),
        in_specs=[pl.BlockSpec((tm, d), lambda i: (i, 0)),
                  pl.BlockSpec((1, d), lambda i: (0, 0)),
                  pl.BlockSpec((d, m), lambda i: (0, 0))],
        out_specs=pl.BlockSpec((tm, m), lambda i: (i, 0)),
        out_shape=jax.ShapeDtypeStruct((n, m), F32),
        compiler_params=_params("parallel"),
        name="norm_matmul",
    )(x, gain.reshape(1, d), w)


def _matmul_residual_kernel(a_ref, w_ref, x_ref, o_ref):
    o_ref[...] = x_ref[...] + _dot(a_ref[...], w_ref[...])


def matmul_residual(a, w, x, tm=512):
    n, k = a.shape
    d = w.shape[1]
    tm = min(tm, n)
    return pl.pallas_call(
        _matmul_residual_kernel,
        grid=(n // tm,),
        in_specs=[pl.BlockSpec((tm, k), lambda i: (i, 0)),
                  pl.BlockSpec((k, d), lambda i: (0, 0)),
                  pl.BlockSpec((tm, d), lambda i: (i, 0))],
        out_specs=pl.BlockSpec((tm, d), lambda i: (i, 0)),
        out_shape=jax.ShapeDtypeStruct((n, d), F32),
        compiler_params=_params("parallel"),
        name="matmul_residual",
    )(a, w, x)


def _even_mixer_kernel(proj_ref, x_ref, vgain_ref, ws_ref, sbt_ref, convw_ref, convb_ref,
                       gbias_ref, hgain_ref, wout_ref, o_ref,
                       zbuf, cext, mstate, mixed, *, n_chunks):
    @pl.when(pl.program_id(1) == 0)
    def _():
        zbuf[0:8, :] = jnp.zeros((8, 2 * ML_W), F32)
        cext[...] = jnp.zeros(cext.shape, F32)
        mstate[...] = jnp.zeros(mstate.shape, F32)

    row = lax.broadcasted_iota(jnp.int32, (CHUNK, CHUNK), 0)
    col = lax.broadcasted_iota(jnp.int32, (CHUNK, CHUNK), 1)
    causal = col <= row
    tri = jnp.where(causal, 1.0, 0.0).astype(F32)
    ones_col = jnp.where(col == 0, 1.0, 0.0).astype(F32)
    q_scale = float(CHUNK) ** -0.5

    for c in range(n_chunks):
        rows = slice(c * CHUNK, (c + 1) * CHUNK)

        for g in range(GM_GROUPS):
            lanes = slice(g * LANES, (g + 1) * LANES)
            u = _gelu(proj_ref[0, rows, lanes])
            v = _gelu(proj_ref[0, rows, GM_W + g * LANES:GM_W + (g + 1) * LANES])
            vn = _rms(v, vgain_ref[:, lanes])
            wm = jnp.where(causal, ws_ref[g], 0.0)
            s = _dot(wm.astype(BF16), vn.astype(BF16)) + sbt_ref[:, g:g + 1]
            mixed[rows, lanes] = (u * s).astype(BF16)

        zbuf[8:8 + CHUNK, :] = proj_ref[0, rows, 2 * GM_W:2 * GM_W + 2 * ML_W]
        conv = convb_ref[...]
        for j in range(CONV_WIDTH):
            lo = 8 - (CONV_WIDTH - 1) + j
            conv = conv + convw_ref[j:j + 1, :] * zbuf[lo:lo + CHUNK, :]
        zbuf[0:8, :] = zbuf[CHUNK:CHUNK + 8, :]
        qk = conv * _sigmoid(conv)

        gt = proj_ref[0, rows, EV_IN_PAD - LANES:EV_IN_PAD] + gbias_ref[...]
        logf = jnp.minimum(gt, 0.0) - jnp.log1p(jnp.exp(-jnp.abs(gt)))
        bcum = jnp.dot(tri, logf, preferred_element_type=F32, precision=lax.Precision.HIGHEST)
        gt_t = gt.T
        bcum_t = bcum.T

        for hd in range(ML_HEADS):
            lanes = slice(hd * LANES, (hd + 1) * LANES)
            q = (qk[:, lanes] * q_scale).astype(BF16)
            kf = qk[:, ML_W + hd * LANES:ML_W + (hd + 1) * LANES]
            k = kf.astype(BF16)
            v = proj_ref[0, rows, 2 * GM_W + 2 * ML_W + hd * LANES:2 * GM_W + 2 * ML_W + (hd + 1) * LANES]
            o_pre = proj_ref[0, rows, 2 * GM_W + 3 * ML_W + hd * LANES:2 * GM_W + 3 * ML_W + (hd + 1) * LANES]
            b_col = bcum[:, ML_HEADS + hd:ML_HEADS + hd + 1]
            b_row = bcum_t[ML_HEADS + hd:ML_HEADS + hd + 1, :]
            i_col = gt[:, hd:hd + 1]
            i_row = gt_t[hd:hd + 1, :]
            m_prev = mstate[hd][:, 0:1]
            c_prev = cext[hd]

            dmat = jnp.where(causal, b_col - b_row + i_row, -jnp.inf)
            inter = b_col + m_prev
            m_t = jnp.maximum(inter, jnp.max(dmat, axis=-1, keepdims=True))
            w_intra = jnp.exp(dmat - m_t)
            w_inter = jnp.exp(inter - m_t)
            sm = _dot_nt(q, k) * w_intra
            v_ext = jnp.concatenate([v, ones_col], axis=1)
            numden = _dot(sm.astype(BF16), v_ext.astype(BF16)) + w_inter * _dot(q, c_prev.astype(BF16))
            num = numden[:, :LANES]
            den = numden[:, LANES:LANES + 1]
            h = num / jnp.maximum(jnp.abs(den), jnp.exp(-m_t))

            b_last = bcum[CHUNK - 1:CHUNK, ML_HEADS + hd:ML_HEADS + hd + 1]
            gdec = b_last - b_col + i_col
            m_new = jnp.maximum(b_last + m_prev, jnp.max(gdec, axis=0, keepdims=True))
            decay = jnp.exp(b_last + m_prev - m_new)
            w_new = jnp.exp(gdec - m_new)
            cext[hd] = decay * c_prev + _dot(kf.T.astype(BF16), (w_new * v_ext).astype(BF16))
            mstate[hd] = jnp.broadcast_to(m_new, (1, LANES))

            hn = _rms(h, hgain_ref[:, lanes])
            mixed[rows, GM_W + hd * LANES:GM_W + (hd + 1) * LANES] = (_sigmoid(o_pre) * hn).astype(BF16)

    o_ref[0] = x_ref[0] + _dot(mixed[...], wout_ref[...])


def even_mixer(proj, x, vgain, ws, sbt, convw, convb, gbias, hgain, wout, tt=256):
    b, s, d = x.shape
    tt = min(tt, s)
    const = lambda shape: pl.BlockSpec(shape, lambda bi, ti: (0,) * len(shape))
    return pl.pallas_call(
        functools.partial(_even_mixer_kernel, n_chunks=tt // CHUNK),
        grid=(b, s // tt),
        in_specs=[pl.BlockSpec((1, tt, EV_IN_PAD), lambda bi, ti: (bi, ti, 0)),
                  pl.BlockSpec((1, tt, d), lambda bi, ti: (bi, ti, 0)),
                  const((1, GM_W)), const((GM_GROUPS, CHUNK, CHUNK)), const((CHUNK, GM_GROUPS)),
                  const((CONV_WIDTH, 2 * ML_W)), const((1, 2 * ML_W)), const((1, LANES)),
                  const((1, ML_W)), const((GM_W + ML_W, d))],
        out_specs=pl.BlockSpec((1, tt, d), lambda bi, ti: (bi, ti, 0)),
        out_shape=jax.ShapeDtypeStruct((b, s, d), F32),
        scratch_shapes=[pltpu.VMEM((CHUNK + 8, 2 * ML_W), F32),
                        pltpu.VMEM((ML_HEADS, CHUNK, 2 * LANES), F32),
                        pltpu.VMEM((ML_HEADS, 1, LANES), F32),
                        pltpu.VMEM((tt, GM_W + ML_W), BF16)],
        compiler_params=_params("arbitrary", "arbitrary"),
        name="even_mixer",
    )(proj, x, vgain, ws, sbt, convw, convb, gbias, hgain, wout)


def _mla_prep_kernel(x_ref, g_ref, win_ref, qg_ref, kvg_ref, wuq_ref, wukv_ref, cs_ref,
                     q_ref, k_ref, v_ref):
    xn = _rms(x_ref[0], g_ref[...])
    proj = _dot(xn.astype(BF16), win_ref[...])
    c_q = _rms(proj[:, :Q_LORA], qg_ref[...])
    c_kv = _rms(proj[:, Q_LORA:Q_LORA + KV_LORA], kvg_ref[...])
    cs = cs_ref[...]
    tm = cs.shape[0]
    lane = lax.broadcasted_iota(jnp.int32, (tm, LANES), 1)
    t = proj[:, Q_LORA + KV_LORA:] * cs
    k_rope = jnp.where(lane < QK_ROPE, t + pltpu.roll(t, QK_ROPE, axis=1), 0.0).astype(BF16)
    q_all = _dot(c_q.astype(BF16), wuq_ref[...])
    kv_all = _dot(c_kv.astype(BF16), wukv_ref[...])
    scale = float(QK_NOPE + QK_ROPE) ** -0.5
    ones_col = jnp.where(lane == 0, 1.0, 0.0).astype(BF16)
    for h in range(MLA_HEADS):
        o = h * QK_PAD
        qt = q_all[:, o + QK_NOPE:o + QK_PAD] * cs
        q_ref[0, h, :, 0:QK_NOPE] = (q_all[:, o:o + QK_NOPE] * scale).astype(BF16)
        q_ref[0, h, :, QK_NOPE:QK_PAD] = ((qt + pltpu.roll(qt, QK_ROPE, axis=1)) * scale).astype(BF16)
        k_ref[0, h, :, 0:QK_NOPE] = kv_all[:, o:o + QK_NOPE].astype(BF16)
        k_ref[0, h, :, QK_NOPE:QK_PAD] = k_rope
        v_ref[0, h, :, 0:V_DIM] = kv_all[:, o + QK_NOPE:o + QK_PAD].astype(BF16)
        v_ref[0, h, :, V_DIM:V_PAD] = ones_col


def mla_prep(x, gain, win, qg, kvg, wuq, wukv, cs, tm=512):
    b, s, d = x.shape
    tm = min(tm, s)
    const = lambda shape: pl.BlockSpec(shape, lambda bi, ti: (0,) * len(shape))
    head_out = lambda w: pl.BlockSpec((1, MLA_HEADS, tm, w), lambda bi, ti: (bi, 0, ti, 0))
    return pl.pallas_call(
        _mla_prep_kernel,
        grid=(b, s // tm),
        in_specs=[pl.BlockSpec((1, tm, d), lambda bi, ti: (bi, ti, 0)),
                  const((1, d)), const(win.shape), const((1, Q_LORA)), const((1, KV_LORA)),
                  const(wuq.shape), const(wukv.shape),
                  pl.BlockSpec((tm, LANES), lambda bi, ti: (ti, 0))],
        out_specs=[head_out(QK_PAD), head_out(QK_PAD), head_out(V_PAD)],
        out_shape=[jax.ShapeDtypeStruct((b, MLA_HEADS, s, QK_PAD), BF16),
                   jax.ShapeDtypeStruct((b, MLA_HEADS, s, QK_PAD), BF16),
                   jax.ShapeDtypeStruct((b, MLA_HEADS, s, V_PAD), BF16)],
        compiler_params=_params("parallel", "parallel"),
        name="mla_prep",
    )(x, gain, win, qg, kvg, wuq, wukv, cs)


def _flash_kernel(qi_ref, ki_ref, q_ref, k_ref, v_ref, o_ref, m_sc, acc_sc, *, blk):
    step = pl.program_id(1)
    qi = qi_ref[step]
    ki = ki_ref[step]

    @pl.when(ki == 0)
    def _():
        m_sc[...] = jnp.full(m_sc.shape, -jnp.inf, F32)
        acc_sc[...] = jnp.zeros(acc_sc.shape, F32)

    def sweep(diagonal):
        def scores(h):
            return _dot_nt(q_ref[0, h], k_ref[0, h])

        s_next = scores(0)
        for h in range(MLA_HEADS):
            s = s_next
            if h + 1 < MLA_HEADS:
                s_next = scores(h + 1)
            if diagonal:
                r = lax.broadcasted_iota(jnp.int32, (blk, blk), 0)
                c = lax.broadcasted_iota(jnp.int32, (blk, blk), 1)
                s = jnp.where(c <= r, s, -jnp.inf)
            m_prev = m_sc[h]
            m_new = jnp.maximum(m_prev, jnp.max(s, axis=1, keepdims=True))
            alpha = jnp.exp(m_prev - m_new)
            p = jnp.exp((s - m_new[:, 0:1]).astype(BF16))
            acc_sc[h] = jnp.concatenate([alpha, alpha], axis=1) * acc_sc[h] + _dot(p, v_ref[0, h])
            m_sc[h] = m_new

    @pl.when(ki < qi)
    def _():
        sweep(False)

    @pl.when(ki == qi)
    def _():
        sweep(True)
        for h in range(MLA_HEADS):
            acc = acc_sc[h]
            o_ref[0, :, h * V_DIM:(h + 1) * V_DIM] = (acc[:, :V_DIM] / acc[:, V_DIM:V_DIM + 1]).astype(BF16)


def flash_attention(q, k, v, blk=512):
    b, nh, s, _ = q.shape
    blk = min(blk, s)
    nq = s // blk
    qi = np.concatenate([np.full(i + 1, i, np.int32) for i in range(nq)])
    ki = np.concatenate([np.arange(i + 1, dtype=np.int32) for i in range(nq)])
    grid_spec = pltpu.PrefetchScalarGridSpec(
        num_scalar_prefetch=2,
        grid=(b, len(qi)),
        in_specs=[pl.BlockSpec((1, nh, blk, QK_PAD), lambda bi, p, qi_r, ki_r: (bi, 0, qi_r[p], 0)),
                  pl.BlockSpec((1, nh, blk, QK_PAD), lambda bi, p, qi_r, ki_r: (bi, 0, ki_r[p], 0)),
                  pl.BlockSpec((1, nh, blk, V_PAD), lambda bi, p, qi_r, ki_r: (bi, 0, ki_r[p], 0))],
        out_specs=pl.BlockSpec((1, blk, nh * V_DIM), lambda bi, p, qi_r, ki_r: (bi, qi_r[p], 0)),
        scratch_shapes=[pltpu.VMEM((nh, blk, LANES), F32),
                        pltpu.VMEM((nh, blk, V_PAD), F32)],
    )
    return pl.pallas_call(
        functools.partial(_flash_kernel, blk=blk),
        grid_spec=grid_spec,
        out_shape=jax.ShapeDtypeStruct((b, s, nh * V_DIM), BF16),
        compiler_params=_params("arbitrary", "arbitrary"),
        name="mla_attention",
    )(jnp.asarray(qi), jnp.asarray(ki), q, k, v)


def _sorting_network(n):
    pairs = []
    p = 1
    while p < n:
        k = p
        while k >= 1:
            for j in range(k % p, n - k, 2 * k):
                for i in range(min(k, n - j - k)):
                    if (i + j) // (2 * p) == (i + j + k) // (2 * p):
                        pairs.append((i + j, i + j + k))
            k //= 2
        p *= 2
    return pairs


def _largest16(tiles, only_last):
    n = PEER_TOPK
    v = list(tiles)

    def exchange(i, j):
        v[i], v[j] = jnp.maximum(v[i], v[j]), jnp.minimum(v[i], v[j])

    for i, j in _sorting_network(n):
        exchange(i, j)
    shift = v[0].shape[0] // 2
    while shift >= 1:
        other = [pltpu.roll(x, shift, axis=0) for x in v]
        v = [jnp.maximum(v[i], other[n - 1 - i]) for i in range(n)]
        if only_last and shift == 1:
            return functools.reduce(jnp.minimum, v)
        stride = n // 2
        while stride >= 1:
            for i in range(n):
                if (i // stride) % 2 == 0:
                    exchange(i, i + stride)
            stride //= 2
        shift //= 2
    return v


def _top16(e):
    sub_rows = e.shape[0] // PEER_TOPK
    v = _largest16([e[r * sub_rows:(r + 1) * sub_rows] for r in range(PEER_TOPK)], only_last=False)
    sub = lax.broadcasted_iota(jnp.int32, v[0].shape, 0)
    halves = []
    for base in range(0, PEER_TOPK, sub_rows):
        rows = v[base]
        for r in range(1, sub_rows):
            rows = jnp.where(sub == r, v[base + r], rows)
        halves.append(rows)
    return jnp.concatenate(halves, axis=0)


def _pair_candidates(ea, eb):
    parts = [ea[0:1] * eb]
    parts += [ea[p:p + 1] * eb[0:8] for p in range(1, 8)]
    parts += [ea[8:16] * eb[0:1]]
    return jnp.concatenate(parts, axis=0)


def _router_kernel(x_ref, g_ref, wq_ref, keys_ref, xn_ref, e1_ref, e2_ref, tau_ref):
    xn_f32 = _rms(x_ref[...], g_ref[...])
    xn_ref[...] = xn_f32.T.astype(BF16)
    q = _dot(xn_f32.astype(BF16), wq_ref[...]).astype(BF16)
    for h in range(PEER_HEADS):
        es, tops = [], []
        for half in range(2):
            j = 2 * h + half
            s_t = _dot_nt(keys_ref[j], q[:, j * PEER_HALF:(j + 1) * PEER_HALF])
            e = jnp.exp(s_t - jnp.max(s_t, axis=0, keepdims=True))
            es.append(e)
            tops.append(_top16(e))
        ea, eb = tops
        cand = _pair_candidates(ea, eb)
        n_tiles = cand.shape[0] // 8
        tiles = [cand[8 * r:8 * (r + 1)] for r in range(n_tiles)]
        tiles += [jnp.zeros_like(tiles[0])] * (PEER_TOPK - n_tiles)
        kth = _largest16(tiles, only_last=True)[0:1]
        sel = cand >= jnp.maximum(kth, 1e-30)
        z = jnp.sum(jnp.where(sel, cand, 0.0), axis=0, keepdims=True)
        scale = 0.5 / z
        cand_z = _pair_candidates(ea * scale, eb)
        tau_ref[h:h + 1, :] = jnp.min(jnp.where(sel, cand_z, jnp.inf), axis=0, keepdims=True)
        e1_ref[h] = es[0] * scale
        e2_ref[h] = es[1]


def peer_router(x, gain, wq, keys, tr=256):
    n, d = x.shape
    tr = min(tr, n)
    return pl.pallas_call(
        _router_kernel,
        grid=(n // tr,),
        in_specs=[pl.BlockSpec((tr, d), lambda i: (i, 0)),
                  pl.BlockSpec((1, d), lambda i: (0, 0)),
                  pl.BlockSpec(wq.shape, lambda i: (0, 0)),
                  pl.BlockSpec(keys.shape, lambda i: (0, 0, 0))],
        out_specs=[pl.BlockSpec((d, tr), lambda i: (0, i)),
                   pl.BlockSpec((PEER_HEADS, N_KEYS, tr), lambda i: (0, 0, i)),
                   pl.BlockSpec((PEER_HEADS, N_KEYS, tr), lambda i: (0, 0, i)),
                   pl.BlockSpec((PEER_HEADS, tr), lambda i: (0, i))],
        out_shape=[jax.ShapeDtypeStruct((d, n), BF16),
                   jax.ShapeDtypeStruct((PEER_HEADS, N_KEYS, n), F32),
                   jax.ShapeDtypeStruct((PEER_HEADS, N_KEYS, n), F32),
                   jax.ShapeDtypeStruct((PEER_HEADS, n), F32)],
        compiler_params=_params("parallel"),
        name="peer_router",
    )(x, gain.reshape(1, d), wq, keys)


def _peer_expert_kernel(xn_ref, x_ref, e1_ref, e2_ref, tau_ref, u_ref, v_ref, o_ref,
                        acc_sc, act_sc, w_sc, *, et, n_sub, tc):
    e = pl.program_id(1)
    tb = xn_ref.shape[1]
    n_i = et // N_KEYS

    @pl.when(e == 0)
    def _():
        acc_sc[...] = jnp.zeros(acc_sc.shape, F32)

    mxu_w = 2 * LANES

    def rows(s):
        return slice(s * et, (s + 1) * et)

    def act_pieces(s):
        def piece(c):
            cols = slice(c * mxu_w, (c + 1) * mxu_w)
            act_sc[s, :, cols] = _dot(u_ref[rows(s), :], xn_ref[:, cols])
        return [functools.partial(piece, c) for c in range(tb // mxu_w)]

    def out_pieces(s):
        def piece(c):
            cols = slice(c * mxu_w, (c + 1) * mxu_w)
            acc_sc[:, cols] += _dot(v_ref[:, rows(s)], w_sc[s, :, cols])
        return [functools.partial(piece, c) for c in range(tb // mxu_w)]

    def build_pieces(s):
        def piece(i, c):
            ig = (e * n_sub + s) * n_i + i
            r = slice(i * N_KEYS, (i + 1) * N_KEYS)
            cols = slice(c * tc, (c + 1) * tc)

            gates = jnp.zeros((N_KEYS, tc), F32)
            for h in range(PEER_HEADS):
                p = e2_ref[h, :, cols] * e1_ref[h, pl.ds(ig, 1), :][:, cols]
                gates = gates + jnp.where(p >= tau_ref[h:h + 1, cols], p, 0.0)
            a = act_sc[s, r, cols]
            two_gelu = a * (1.0 + jnp.tanh(a * (GELU_C1 + GELU_C3 * (a * a))))
            w_sc[s, r, cols] = (gates * two_gelu).astype(BF16)
        return [functools.partial(piece, i, c) for i in range(n_i) for c in range(tb // tc)]

    def interleave(vector_work, matrix_work):
        done = 0
        for k, piece in enumerate(vector_work):
            piece()
            due = (k + 1) * len(matrix_work) // len(vector_work)
            for m in matrix_work[done:due]:
                m()
            done = due

    for m in act_pieces(0):
        m()
    for s in range(n_sub):
        matrix_work = (act_pieces(s + 1) if s + 1 < n_sub else []) + (out_pieces(s - 1) if s > 0 else [])
        interleave(build_pieces(s), matrix_work)
    for m in out_pieces(n_sub - 1):
        m()

    @pl.when(e == pl.num_programs(1) - 1)
    def _():
        o_ref[...] = x_ref[...] + acc_sc[...].T


def peer_experts(xn, x, e1, e2, tau, u, v, tb=512, et=512, n_sub=4, tc=128):
    n, d = x.shape
    ne = u.shape[0]
    tb = min(tb, n)
    tc = min(tc, tb)
    blk_e = et * n_sub
    per_key = pl.BlockSpec((PEER_HEADS, N_KEYS, tb), lambda i, j: (0, 0, i))
    return pl.pallas_call(
        functools.partial(_peer_expert_kernel, et=et, n_sub=n_sub, tc=tc),
        grid=(n // tb, ne // blk_e),
        in_specs=[pl.BlockSpec((d, tb), lambda i, j: (0, i)),
                  pl.BlockSpec((tb, d), lambda i, j: (i, 0)),
                  per_key, per_key, pl.BlockSpec((PEER_HEADS, tb), lambda i, j: (0, i)),
                  pl.BlockSpec((blk_e, d), lambda i, j: (j, 0)),
                  pl.BlockSpec((d, blk_e), lambda i, j: (0, j))],
        out_specs=pl.BlockSpec((tb, d), lambda i, j: (i, 0)),
        out_shape=jax.ShapeDtypeStruct((n, d), F32),
        scratch_shapes=[pltpu.VMEM((d, tb), F32),
                        pltpu.VMEM((n_sub, et, tb), F32),
                        pltpu.VMEM((n_sub, et, tb), BF16)],
        compiler_params=_params("parallel", "arbitrary"),
        name="peer_experts",
    )(xn, x, e1, e2, tau, u, v)


def _final_norm_kernel(x_ref, g_ref, o_ref):
    o_ref[...] = _rms(x_ref[...], g_ref[...])


def final_norm(x, gain, tm=1024):
    n, d = x.shape
    tm = min(tm, n)
    return pl.pallas_call(
        _final_norm_kernel,
        grid=(n // tm,),
        in_specs=[pl.BlockSpec((tm, d), lambda i: (i, 0)), pl.BlockSpec((1, d), lambda i: (0, 0))],
        out_specs=pl.BlockSpec((tm, d), lambda i: (i, 0)),
        out_shape=jax.ShapeDtypeStruct((n, d), F32),
        compiler_params=_params("parallel"),
        name="final_norm",
    )(x, gain.reshape(1, d))


def _swap_halves(w):
    half = w.shape[-1] // 2
    return jnp.concatenate([w[..., half:], w[..., :half]], axis=-1)


def _even_layer(x, gain, w_in, v_gain, ws, sb, conv_w, conv_b, i_bias, f_bias, h_gain, w_out):
    b, s, d = x.shape
    w_in_p = jnp.pad(w_in, ((0, 0), (0, EV_IN_PAD - EV_IN))).astype(BF16)
    proj = norm_matmul(x.reshape(b * s, d), gain, w_in_p).reshape(b, s, EV_IN_PAD)
    gbias = jnp.pad(jnp.concatenate([i_bias, f_bias]), (0, LANES - 2 * ML_HEADS)).reshape(1, LANES)
    return even_mixer(proj, x, v_gain.reshape(1, GM_W), ws, sb.T, conv_w, conv_b.reshape(1, 2 * ML_W),
                      gbias, h_gain.reshape(1, ML_W), w_out.astype(BF16))


def _odd_layer(x, gain, w_in, q_gain, kv_gain, w_uq, w_ukv, w_out, cs):
    b, s, d = x.shape
    rope_cols = w_in[:, Q_LORA + KV_LORA:]
    win = jnp.concatenate([w_in, _swap_halves(rope_cols)], axis=1).astype(BF16)
    wq = w_uq.reshape(Q_LORA, MLA_HEADS, QK_NOPE + QK_ROPE)
    wuq = jnp.concatenate([wq, _swap_halves(wq[..., QK_NOPE:])], axis=-1)
    wuq = wuq.reshape(Q_LORA, MLA_HEADS * QK_PAD).astype(BF16)
    q, k, v = mla_prep(x, gain.reshape(1, d), win, q_gain.reshape(1, Q_LORA), kv_gain.reshape(1, KV_LORA),
                       wuq, w_ukv.astype(BF16), cs)
    o = flash_attention(q, k, v)
    return matmul_residual(o.reshape(b * s, MLA_HEADS * V_DIM), w_out.astype(BF16),
                           x.reshape(b * s, d)).reshape(b, s, d)


def _peer_layer(x, gain, w_q, keys, u_tab, v_tab):
    b, s, d = x.shape
    x2 = x.reshape(b * s, d)
    keys16 = keys.reshape(PEER_HEADS * 2, N_KEYS, PEER_HALF).astype(BF16)
    xn, e1, e2, tau = peer_router(x2, gain, w_q.astype(BF16), keys16)
    out = peer_experts(xn, x2, e1, e2, tau, u_tab.astype(BF16), v_tab.astype(BF16).T)
    return out.reshape(b, s, d)


def kernel(x, norm_mix, norm_ffn, norm_final, ev_w_in, ev_gm_v_gain, ev_gm_ws, ev_gm_b, ev_conv_w, ev_conv_b, ev_i_bias, ev_f_bias, ev_h_gain, ev_w_out, od_w_in, od_q_gain, od_kv_gain, od_w_uq, od_w_ukv, od_w_out, peer_w_q, peer_keys, peer_u, peer_v):
    b, s, d = x.shape
    depth = norm_mix.shape[0]
    pos = jnp.arange(s, dtype=F32)
    inv_freq = ROPE_THETA ** (-jnp.arange(QK_ROPE // 2, dtype=F32) / (QK_ROPE // 2))
    ang = pos[:, None] * inv_freq[None, :]
    cos, sin = jnp.cos(ang), jnp.sin(ang)
    cs = jnp.concatenate([cos, cos, -sin, sin], axis=1)
    for layer in range(depth):
        j = layer // 2
        if layer % 2 == 0:
            x = _even_layer(x, norm_mix[layer], ev_w_in[j], ev_gm_v_gain[j], ev_gm_ws[j], ev_gm_b[j],
                            ev_conv_w[j], ev_conv_b[j], ev_i_bias[j], ev_f_bias[j], ev_h_gain[j], ev_w_out[j])
        else:
            x = _odd_layer(x, norm_mix[layer], od_w_in[j], od_q_gain[j], od_kv_gain[j], od_w_uq[j],
                           od_w_ukv[j], od_w_out[j], cs)
        x = _peer_layer(x, norm_ffn[layer], peer_w_q[layer], peer_keys[layer], peer_u[layer], peer_v[layer])
    return final_norm(x.reshape(b * s, d), norm_final).reshape(b, s, d)
```

```python
import functools
import math

import jax
import jax.numpy as jnp
import numpy as np
from jax import lax
from jax.experimental import pallas as pl
from jax.experimental.pallas import tpu as pltpu

F32 = jnp.float32
BF16 = jnp.bfloat16
EPS = 1e-6
LANES = 128
GELU_C1 = math.sqrt(2.0 / math.pi)
GELU_C3 = 0.044715 * GELU_C1
VMEM_LIMIT = 56 * 1024 * 1024

GM_GROUPS = 4
GM_W = 512
ML_HEADS = 4
ML_W = 512
CHUNK = 128
CONV_WIDTH = 4
EV_IN = 2 * GM_W + 4 * ML_W + 2 * ML_HEADS
EV_IN_PAD = 2 * GM_W + 4 * ML_W + LANES
MLA_HEADS = 8
QK_NOPE = 128
QK_ROPE = 64
V_DIM = 128
Q_LORA = 512
KV_LORA = 256
ROPE_THETA = 10000.0
QK_PAD = 256
V_PAD = 256
PEER_HEADS = 8
N_KEYS = 128
PEER_TOPK = 16
PEER_HALF = 128


def _params(*sem):
    return pltpu.CompilerParams(dimension_semantics=sem, vmem_limit_bytes=VMEM_LIMIT)


def _rms(x, gain):
    return x * lax.rsqrt(jnp.mean(x * x, axis=-1, keepdims=True) + EPS) * gain


def _gelu(x):
    c = math.sqrt(2.0 / math.pi)
    return x * (0.5 * (1.0 + jnp.tanh(c * (x + 0.044715 * (x * x * x)))))


def _sigmoid(x):
    return 1.0 / (1.0 + jnp.exp(-x))


def _dot(a, b):
    return jnp.dot(a, b, preferred_element_type=F32)


def _dot_nt(a, b):
    return lax.dot_general(a, b, (((1,), (1,)), ((), ())), preferred_element_type=F32)


def _dot_tn(a, b):
    return lax.dot_general(a, b, (((0,), (0,)), ((), ())), preferred_element_type=F32)


def _norm_matmul_kernel(x_ref, g_ref, w_ref, o_ref):
    xn = _rms(x_ref[...], g_ref[...])
    o_ref[...] = _dot(xn.astype(BF16), w_ref[...])


def norm_matmul(x, gain, w, tm=512):
    n, d = x.shape
    m = w.shape[1]
    tm = min(tm, n)
    return pl.pallas_call(
        _norm_matmul_kernel,
        grid=(n // tm,),
        in_specs=[pl.BlockSpec((tm, d), lambda i: (i, 0)),
                  pl.BlockSpec((1, d), lambda i: (0, 0)),
                  pl.BlockSpec((d, m), lambda i: (0, 0))],
        out_specs=pl.BlockSpec((tm, m), lambda i: (i, 0)),
        out_shape=jax.ShapeDtypeStruct((n, m), F32),
        compiler_params=_params("parallel"),
        name="norm_matmul",
    )(x, gain.reshape(1, d), w)


def _matmul_residual_kernel(a_ref, w_ref, x_ref, o_ref):
    o_ref[...] = x_ref[...] + _dot(a_ref[...], w_ref[...])


def matmul_residual(a, w, x, tm=1024):
    n, k = a.shape
    d = w.shape[1]
    tm = min(tm, n)
    return pl.pallas_call(
        _matmul_residual_kernel,
        grid=(n // tm,),
        in_specs=[pl.BlockSpec((tm, k), lambda i: (i, 0)),
                  pl.BlockSpec((k, d), lambda i: (0, 0)),
                  pl.BlockSpec((tm, d), lambda i: (i, 0))],
        out_specs=pl.BlockSpec((tm, d), lambda i: (i, 0)),
        out_shape=jax.ShapeDtypeStruct((n, d), F32),
        compiler_params=_params("parallel"),
        name="matmul_residual",
    )(a, w, x)


def _even_mixer_kernel(proj_ref, x_ref, vgain_ref, ws_ref, sbt_ref, convw_ref, convb_ref,
                       gbias_ref, hgain_ref, wout_ref, o_ref,
                       zbuf, cext, mstate, mixed, *, n_chunks):
    @pl.when(pl.program_id(1) == 0)
    def _():
        zbuf[0:8, :] = jnp.zeros((8, 2 * ML_W), F32)
        cext[...] = jnp.zeros(cext.shape, F32)
        mstate[...] = jnp.zeros(mstate.shape, F32)

    row = lax.broadcasted_iota(jnp.int32, (CHUNK, CHUNK), 0)
    col = lax.broadcasted_iota(jnp.int32, (CHUNK, CHUNK), 1)
    causal = col <= row
    tri = jnp.where(causal, 1.0, 0.0).astype(F32)
    ones_col = jnp.where(col == 0, 1.0, 0.0).astype(F32)
    q_scale = float(CHUNK) ** -0.5

    for c in range(n_chunks):
        rows = slice(c * CHUNK, (c + 1) * CHUNK)

        for g in range(GM_GROUPS):
            lanes = slice(g * LANES, (g + 1) * LANES)
            u = _gelu(proj_ref[0, rows, lanes])
            v = _gelu(proj_ref[0, rows, GM_W + g * LANES:GM_W + (g + 1) * LANES])
            vn = _rms(v, vgain_ref[:, lanes])
            wm = jnp.where(causal, ws_ref[g], 0.0)
            s = _dot(wm.astype(BF16), vn.astype(BF16)) + sbt_ref[:, g:g + 1]
            mixed[rows, lanes] = (u * s).astype(BF16)

        zbuf[8:8 + CHUNK, :] = proj_ref[0, rows, 2 * GM_W:2 * GM_W + 2 * ML_W]
        conv = convb_ref[...]
        for j in range(CONV_WIDTH):
            lo = 8 - (CONV_WIDTH - 1) + j
            conv = conv + convw_ref[j:j + 1, :] * zbuf[lo:lo + CHUNK, :]
        zbuf[0:8, :] = zbuf[CHUNK:CHUNK + 8, :]
        qk = conv * _sigmoid(conv)

        gt = proj_ref[0, rows, EV_IN_PAD - LANES:EV_IN_PAD] + gbias_ref[...]
        logf = jnp.minimum(gt, 0.0) - jnp.log1p(jnp.exp(-jnp.abs(gt)))
        bcum = jnp.dot(tri, logf, preferred_element_type=F32, precision=lax.Precision.HIGHEST)
        gt_t = gt.T
        bcum_t = bcum.T

        for hd in range(ML_HEADS):
            lanes = slice(hd * LANES, (hd + 1) * LANES)
            q = (qk[:, lanes] * q_scale).astype(BF16)
            kf = qk[:, ML_W + hd * LANES:ML_W + (hd + 1) * LANES]
            k = kf.astype(BF16)
            v = proj_ref[0, rows, 2 * GM_W + 2 * ML_W + hd * LANES:2 * GM_W + 2 * ML_W + (hd + 1) * LANES]
            o_pre = proj_ref[0, rows, 2 * GM_W + 3 * ML_W + hd * LANES:2 * GM_W + 3 * ML_W + (hd + 1) * LANES]
            b_col = bcum[:, ML_HEADS + hd:ML_HEADS + hd + 1]
            b_row = bcum_t[ML_HEADS + hd:ML_HEADS + hd + 1, :]
            i_col = gt[:, hd:hd + 1]
            i_row = gt_t[hd:hd + 1, :]
            m_prev = mstate[hd][:, 0:1]
            c_prev = cext[hd]

            dmat = jnp.where(causal, b_col - b_row + i_row, -jnp.inf)
            inter = b_col + m_prev
            m_t = jnp.maximum(inter, jnp.max(dmat, axis=-1, keepdims=True))
            w_intra = jnp.exp(dmat - m_t)
            w_inter = jnp.exp(inter - m_t)
            sm = _dot_nt(q, k) * w_intra
            v_ext = jnp.concatenate([v, ones_col], axis=1)
            numden = _dot(sm.astype(BF16), v_ext.astype(BF16)) + w_inter * _dot(q, c_prev.astype(BF16))
            num = numden[:, :LANES]
            den = numden[:, LANES:LANES + 1]
            h = num / jnp.maximum(jnp.abs(den), jnp.exp(-m_t))

            b_last = bcum[CHUNK - 1:CHUNK, ML_HEADS + hd:ML_HEADS + hd + 1]
            gdec = b_last - b_col + i_col
            m_new = jnp.maximum(b_last + m_prev, jnp.max(gdec, axis=0, keepdims=True))
            decay = jnp.exp(b_last + m_prev - m_new)
            w_new = jnp.exp(gdec - m_new)
            cext[hd] = decay * c_prev + _dot(kf.T.astype(BF16), (w_new * v_ext).astype(BF16))
            mstate[hd] = jnp.broadcast_to(m_new, (1, LANES))

            hn = _rms(h, hgain_ref[:, lanes])
            mixed[rows, GM_W + hd * LANES:GM_W + (hd + 1) * LANES] = (_sigmoid(o_pre) * hn).astype(BF16)

    o_ref[0] = x_ref[0] + _dot(mixed[...], wout_ref[...])


def even_mixer(proj, x, vgain, ws, sbt, convw, convb, gbias, hgain, wout, tt=512):
    b, s, d = x.shape
    tt = min(tt, s)
    const = lambda shape: pl.BlockSpec(shape, lambda bi, ti: (0,) * len(shape))
    return pl.pallas_call(
        functools.partial(_even_mixer_kernel, n_chunks=tt // CHUNK),
        grid=(b, s // tt),
        in_specs=[pl.BlockSpec((1, tt, EV_IN_PAD), lambda bi, ti: (bi, ti, 0)),
                  pl.BlockSpec((1, tt, d), lambda bi, ti: (bi, ti, 0)),
                  const((1, GM_W)), const((GM_GROUPS, CHUNK, CHUNK)), const((CHUNK, GM_GROUPS)),
                  const((CONV_WIDTH, 2 * ML_W)), const((1, 2 * ML_W)), const((1, LANES)),
                  const((1, ML_W)), const((GM_W + ML_W, d))],
        out_specs=pl.BlockSpec((1, tt, d), lambda bi, ti: (bi, ti, 0)),
        out_shape=jax.ShapeDtypeStruct((b, s, d), F32),
        scratch_shapes=[pltpu.VMEM((CHUNK + 8, 2 * ML_W), F32),
                        pltpu.VMEM((ML_HEADS, CHUNK, 2 * LANES), F32),
                        pltpu.VMEM((ML_HEADS, 1, LANES), F32),
                        pltpu.VMEM((tt, GM_W + ML_W), BF16)],
        compiler_params=_params("arbitrary", "arbitrary"),
        name="even_mixer",
    )(proj, x, vgain, ws, sbt, convw, convb, gbias, hgain, wout)


def _mla_prep_kernel(x_ref, g_ref, win_ref, qg_ref, kvg_ref, wuq_ref, wukv_ref, cs_ref,
                     q_ref, k_ref, v_ref):
    xn = _rms(x_ref[0], g_ref[...])
    proj = _dot(xn.astype(BF16), win_ref[...])
    c_q = _rms(proj[:, :Q_LORA], qg_ref[...])
    c_kv = _rms(proj[:, Q_LORA:Q_LORA + KV_LORA], kvg_ref[...])
    cs = cs_ref[...]
    tm = cs.shape[0]
    lane = lax.broadcasted_iota(jnp.int32, (tm, LANES), 1)
    t = proj[:, Q_LORA + KV_LORA:] * cs
    k_rope_t = jnp.where(lane < QK_ROPE, t + pltpu.roll(t, QK_ROPE, axis=1), 0.0).T.astype(BF16)
    q_all = _dot(c_q.astype(BF16), wuq_ref[...])
    kv_all = _dot(c_kv.astype(BF16), wukv_ref[...])
    scale = float(QK_NOPE + QK_ROPE) ** -0.5
    ones_col = jnp.where(lane == 0, 1.0, 0.0).astype(BF16)
    for h in range(MLA_HEADS):
        o = h * QK_PAD
        qt = q_all[:, o + QK_NOPE:o + QK_PAD] * cs
        q_ref[0, h, :, 0:QK_NOPE] = (q_all[:, o:o + QK_NOPE] * scale).astype(BF16)
        q_ref[0, h, :, QK_NOPE:QK_PAD] = ((qt + pltpu.roll(qt, QK_ROPE, axis=1)) * scale).astype(BF16)
        k_ref[0, h, 0:QK_NOPE, :] = kv_all[:, o:o + QK_NOPE].T.astype(BF16)
        k_ref[0, h, QK_NOPE:QK_PAD, :] = k_rope_t
        v_ref[0, h, :, 0:V_DIM] = kv_all[:, o + QK_NOPE:o + QK_PAD].astype(BF16)
        v_ref[0, h, :, V_DIM:V_PAD] = ones_col


def mla_prep(x, gain, win, qg, kvg, wuq, wukv, cs, tm=512):
    b, s, d = x.shape
    tm = min(tm, s)
    const = lambda shape: pl.BlockSpec(shape, lambda bi, ti: (0,) * len(shape))
    head_out = lambda w: pl.BlockSpec((1, MLA_HEADS, tm, w), lambda bi, ti: (bi, 0, ti, 0))
    return pl.pallas_call(
        _mla_prep_kernel,
        grid=(b, s // tm),
        in_specs=[pl.BlockSpec((1, tm, d), lambda bi, ti: (bi, ti, 0)),
                  const((1, d)), const(win.shape), const((1, Q_LORA)), const((1, KV_LORA)),
                  const(wuq.shape), const(wukv.shape),
                  pl.BlockSpec((tm, LANES), lambda bi, ti: (ti, 0))],
        out_specs=[head_out(QK_PAD),
                   pl.BlockSpec((1, MLA_HEADS, QK_PAD, tm), lambda bi, ti: (bi, 0, 0, ti)),
                   head_out(V_PAD)],
        out_shape=[jax.ShapeDtypeStruct((b, MLA_HEADS, s, QK_PAD), BF16),
                   jax.ShapeDtypeStruct((b, MLA_HEADS, QK_PAD, s), BF16),
                   jax.ShapeDtypeStruct((b, MLA_HEADS, s, V_PAD), BF16)],
        compiler_params=_params("parallel", "parallel"),
        name="mla_prep",
    )(x, gain, win, qg, kvg, wuq, wukv, cs)


def _flash_kernel(qi_ref, ki_ref, q_ref, k_ref, v_ref, o_ref, m_sc, acc_sc, *, blk):
    step = pl.program_id(1)
    qi = qi_ref[step]
    ki = ki_ref[step]

    @pl.when(ki == 0)
    def _():
        m_sc[...] = jnp.full(m_sc.shape, -jnp.inf, F32)
        acc_sc[...] = jnp.zeros(acc_sc.shape, F32)

    def sweep(diagonal):
        def scores(h):
            return _dot(q_ref[0, h], k_ref[0, h])

        if diagonal:
            r = lax.broadcasted_iota(jnp.int32, (blk, blk), 0)
            c = lax.broadcasted_iota(jnp.int32, (blk, blk), 1)
            causal_bias = jnp.where(c <= r, 0.0, -jnp.inf)
        s_next = scores(0)
        for h in range(MLA_HEADS):
            s = s_next
            if h + 1 < MLA_HEADS:
                s_next = scores(h + 1)
            if diagonal:
                s = s + causal_bias
            m_prev = m_sc[h]
            m_new = jnp.maximum(m_prev, jnp.max(s, axis=1, keepdims=True))
            alpha = jnp.exp(m_prev - m_new)
            p = jnp.exp((s - m_new[:, 0:1]).astype(BF16))
            acc_sc[h] = jnp.concatenate([alpha, alpha], axis=1) * acc_sc[h] + _dot(p, v_ref[0, h])
            m_sc[h] = m_new

    @pl.when(ki < qi)
    def _():
        sweep(False)

    @pl.when(ki == qi)
    def _():
        sweep(True)
        for h in range(MLA_HEADS):
            acc = acc_sc[h]
            o_ref[0, :, h * V_DIM:(h + 1) * V_DIM] = (acc[:, :V_DIM] / acc[:, V_DIM:V_DIM + 1]).astype(BF16)


def flash_attention(q, k, v, blk=512):
    b, nh, s, _ = q.shape
    blk = min(blk, s)
    nq = s // blk
    qi = np.concatenate([np.full(i + 1, i, np.int32) for i in range(nq)])
    ki = np.concatenate([np.arange(i + 1, dtype=np.int32) for i in range(nq)])
    grid_spec = pltpu.PrefetchScalarGridSpec(
        num_scalar_prefetch=2,
        grid=(b, len(qi)),
        in_specs=[pl.BlockSpec((1, nh, blk, QK_PAD), lambda bi, p, qi_r, ki_r: (bi, 0, qi_r[p], 0)),
                  pl.BlockSpec((1, nh, QK_PAD, blk), lambda bi, p, qi_r, ki_r: (bi, 0, 0, ki_r[p])),
                  pl.BlockSpec((1, nh, blk, V_PAD), lambda bi, p, qi_r, ki_r: (bi, 0, ki_r[p], 0))],
        out_specs=pl.BlockSpec((1, blk, nh * V_DIM), lambda bi, p, qi_r, ki_r: (bi, qi_r[p], 0)),
        scratch_shapes=[pltpu.VMEM((nh, blk, LANES), F32),
                        pltpu.VMEM((nh, blk, V_PAD), F32)],
    )
    return pl.pallas_call(
        functools.partial(_flash_kernel, blk=blk),
        grid_spec=grid_spec,
        out_shape=jax.ShapeDtypeStruct((b, s, nh * V_DIM), BF16),
        compiler_params=_params("arbitrary", "arbitrary"),
        name="mla_attention",
    )(jnp.asarray(qi), jnp.asarray(ki), q, k, v)


def _sorting_network(n):
    pairs = []
    p = 1
    while p < n:
        k = p
        while k >= 1:
            for j in range(k % p, n - k, 2 * k):
                for i in range(min(k, n - j - k)):
                    if (i + j) // (2 * p) == (i + j + k) // (2 * p):
                        pairs.append((i + j, i + j + k))
            k //= 2
        p *= 2
    return pairs


def _largest16(tiles, only_last):
    n = PEER_TOPK
    v = list(tiles)

    def exchange(i, j):
        v[i], v[j] = jnp.maximum(v[i], v[j]), jnp.minimum(v[i], v[j])

    for i, j in _sorting_network(n):
        exchange(i, j)
    shift = v[0].shape[0] // 2
    while shift >= 1:
        other = [pltpu.roll(x, shift, axis=0) for x in v]
        v = [jnp.maximum(v[i], other[n - 1 - i]) for i in range(n)]
        if only_last and shift == 1:
            return functools.reduce(jnp.minimum, v)
        stride = n // 2
        while stride >= 1:
            for i in range(n):
                if (i // stride) % 2 == 0:
                    exchange(i, i + stride)
            stride //= 2
        shift //= 2
    return v


def _top16(e):
    sub_rows = e.shape[0] // PEER_TOPK
    v = _largest16([e[r * sub_rows:(r + 1) * sub_rows] for r in range(PEER_TOPK)], only_last=False)
    sub = lax.broadcasted_iota(jnp.int32, v[0].shape, 0)
    halves = []
    for base in range(0, PEER_TOPK, sub_rows):
        rows = v[base]
        for r in range(1, sub_rows):
            rows = jnp.where(sub == r, v[base + r], rows)
        halves.append(rows)
    return jnp.concatenate(halves, axis=0)


def _pair_candidates(ea, eb):
    parts = [ea[0:1] * eb]
    parts += [ea[p:p + 1] * eb[0:8] for p in range(1, 8)]
    parts += [ea[8:16] * eb[0:1]]
    return jnp.concatenate(parts, axis=0)


def _router_kernel(x_ref, g_ref, wq_ref, keys_ref, xn_ref, e1_ref, e2_ref, tau_ref):
    xn_f32 = _rms(x_ref[...], g_ref[...])
    xn_ref[...] = xn_f32.T.astype(BF16)
    q = _dot(xn_f32.astype(BF16), wq_ref[...]).astype(BF16)
    for h in range(PEER_HEADS):
        es, tops = [], []
        for half in range(2):
            j = 2 * h + half
            s_t = _dot_nt(keys_ref[j], q[:, j * PEER_HALF:(j + 1) * PEER_HALF])
            e = jnp.exp(s_t - jnp.max(s_t, axis=0, keepdims=True))
            es.append(e)
            tops.append(_top16(e))
        ea, eb = tops
        cand = _pair_candidates(ea, eb)
        n_tiles = cand.shape[0] // 8
        tiles = [cand[8 * r:8 * (r + 1)] for r in range(n_tiles)]
        tiles += [jnp.zeros_like(tiles[0])] * (PEER_TOPK - n_tiles)
        kth = _largest16(tiles, only_last=True)[0:1]
        sel = cand >= jnp.maximum(kth, 1e-30)
        z = jnp.sum(jnp.where(sel, cand, 0.0), axis=0, keepdims=True)
        scale = 0.5 / z
        cand_z = _pair_candidates(ea * scale, eb)
        tau_ref[h:h + 1, :] = jnp.min(jnp.where(sel, cand_z, jnp.inf), axis=0, keepdims=True)
        e1_ref[h] = es[0] * scale
        e2_ref[h] = es[1]


def peer_router(x, gain, wq, keys, tr=256):
    n, d = x.shape
    tr = min(tr, n)
    return pl.pallas_call(
        _router_kernel,
        grid=(n // tr,),
        in_specs=[pl.BlockSpec((tr, d), lambda i: (i, 0)),
                  pl.BlockSpec((1, d), lambda i: (0, 0)),
                  pl.BlockSpec(wq.shape, lambda i: (0, 0)),
                  pl.BlockSpec(keys.shape, lambda i: (0, 0, 0))],
        out_specs=[pl.BlockSpec((d, tr), lambda i: (0, i)),
                   pl.BlockSpec((PEER_HEADS, N_KEYS, tr), lambda i: (0, 0, i)),
                   pl.BlockSpec((PEER_HEADS, N_KEYS, tr), lambda i: (0, 0, i)),
                   pl.BlockSpec((PEER_HEADS, tr), lambda i: (0, i))],
        out_shape=[jax.ShapeDtypeStruct((d, n), BF16),
                   jax.ShapeDtypeStruct((PEER_HEADS, N_KEYS, n), F32),
                   jax.ShapeDtypeStruct((PEER_HEADS, N_KEYS, n), F32),
                   jax.ShapeDtypeStruct((PEER_HEADS, n), F32)],
        compiler_params=_params("parallel"),
        name="peer_router",
    )(x, gain.reshape(1, d), wq, keys)


def _peer_expert_kernel(xn_ref, x_ref, e1_ref, e2_ref, tau_ref, u_ref, v_ref, o_ref,
                        acc_sc, act_sc, w_sc, *, et, n_sub, tc):
    e = pl.program_id(1)
    tb = xn_ref.shape[1]
    n_i = et // N_KEYS

    @pl.when(e == 0)
    def _():
        acc_sc[...] = jnp.zeros(acc_sc.shape, F32)

    mxu_w = 2 * LANES

    def rows(s):
        return slice(s * et, (s + 1) * et)

    def act_pieces(s):
        def piece(c):
            cols = slice(c * mxu_w, (c + 1) * mxu_w)
            act_sc[s, :, cols] = _dot(u_ref[rows(s), :], xn_ref[:, cols])
        return [functools.partial(piece, c) for c in range(tb // mxu_w)]

    def out_pieces(s):
        def piece(c):
            cols = slice(c * mxu_w, (c + 1) * mxu_w)
            acc_sc[:, cols] += _dot(v_ref[:, rows(s)], w_sc[s, :, cols])
        return [functools.partial(piece, c) for c in range(tb // mxu_w)]

    def build_pieces(s):
        def piece(i, c):
            ig = (e * n_sub + s) * n_i + i
            r = slice(i * N_KEYS, (i + 1) * N_KEYS)
            cols = slice(c * tc, (c + 1) * tc)

            gates = jnp.zeros((N_KEYS, tc), F32)
            for h in range(PEER_HEADS):
                p = e2_ref[h, :, cols] * e1_ref[h, pl.ds(ig, 1), :][:, cols]
                gates = gates + jnp.where(p >= tau_ref[h:h + 1, cols], p, 0.0)
            a = act_sc[s, r, cols]
            two_gelu = a * (1.0 + jnp.tanh(a * (GELU_C1 + GELU_C3 * (a * a))))
            w_sc[s, r, cols] = (gates * two_gelu).astype(BF16)
        return [functools.partial(piece, i, c) for i in range(n_i) for c in range(tb // tc)]

    def interleave(vector_work, matrix_work):
        done = 0
        for k, piece in enumerate(vector_work):
            piece()
            due = (k + 1) * len(matrix_work) // len(vector_work)
            for m in matrix_work[done:due]:
                m()
            done = due

    for m in act_pieces(0):
        m()
    for s in range(n_sub):
        matrix_work = (act_pieces(s + 1) if s + 1 < n_sub else []) + (out_pieces(s - 1) if s > 0 else [])
        interleave(build_pieces(s), matrix_work)
    for m in out_pieces(n_sub - 1):
        m()

    @pl.when(e == pl.num_programs(1) - 1)
    def _():
        o_ref[...] = x_ref[...] + acc_sc[...].T


def peer_experts(xn, x, e1, e2, tau, u, v, tb=512, et=512, n_sub=4, tc=128):
    n, d = x.shape
    ne = u.shape[0]
    tb = min(tb, n)
    tc = min(tc, tb)
    blk_e = et * n_sub
    per_key = pl.BlockSpec((PEER_HEADS, N_KEYS, tb), lambda i, j: (0, 0, i))
    return pl.pallas_call(
        functools.partial(_peer_expert_kernel, et=et, n_sub=n_sub, tc=tc),
        grid=(n // tb, ne // blk_e),
        in_specs=[pl.BlockSpec((d, tb), lambda i, j: (0, i)),
                  pl.BlockSpec((tb, d), lambda i, j: (i, 0)),
                  per_key, per_key, pl.BlockSpec((PEER_HEADS, tb), lambda i, j: (0, i)),
                  pl.BlockSpec((blk_e, d), lambda i, j: (j, 0)),
                  pl.BlockSpec((d, blk_e), lambda i, j: (0, j))],
        out_specs=pl.BlockSpec((tb, d), lambda i, j: (i, 0)),
        out_shape=jax.ShapeDtypeStruct((n, d), F32),
        scratch_shapes=[pltpu.VMEM((d, tb), F32),
                        pltpu.VMEM((n_sub, et, tb), F32),
                        pltpu.VMEM((n_sub, et, tb), BF16)],
        compiler_params=_params("parallel", "arbitrary"),
        name="peer_experts",
    )(xn, x, e1, e2, tau, u, v)


def _final_norm_kernel(x_ref, g_ref, o_ref):
    o_ref[...] = _rms(x_ref[...], g_ref[...])


def final_norm(x, gain, tm=1024):
    n, d = x.shape
    tm = min(tm, n)
    return pl.pallas_call(
        _final_norm_kernel,
        grid=(n // tm,),
        in_specs=[pl.BlockSpec((tm, d), lambda i: (i, 0)), pl.BlockSpec((1, d), lambda i: (0, 0))],
        out_specs=pl.BlockSpec((tm, d), lambda i: (i, 0)),
        out_shape=jax.ShapeDtypeStruct((n, d), F32),
        compiler_params=_params("parallel"),
        name="final_norm",
    )(x, gain.reshape(1, d))


def _swap_halves(w):
    half = w.shape[-1] // 2
    return jnp.concatenate([w[..., half:], w[..., :half]], axis=-1)


def _even_layer(x, gain, w_in, v_gain, ws, sb, conv_w, conv_b, i_bias, f_bias, h_gain, w_out):
    b, s, d = x.shape
    w_in_p = jnp.pad(w_in, ((0, 0), (0, EV_IN_PAD - EV_IN))).astype(BF16)
    proj = norm_matmul(x.reshape(b * s, d), gain, w_in_p).reshape(b, s, EV_IN_PAD)
    gbias = jnp.pad(jnp.concatenate([i_bias, f_bias]), (0, LANES - 2 * ML_HEADS)).reshape(1, LANES)
    return even_mixer(proj, x, v_gain.reshape(1, GM_W), ws, sb.T, conv_w, conv_b.reshape(1, 2 * ML_W),
                      gbias, h_gain.reshape(1, ML_W), w_out.astype(BF16))


def _odd_layer(x, gain, w_in, q_gain, kv_gain, w_uq, w_ukv, w_out, cs):
    b, s, d = x.shape
    rope_cols = w_in[:, Q_LORA + KV_LORA:]
    win = jnp.concatenate([w_in, _swap_halves(rope_cols)], axis=1).astype(BF16)
    wq = w_uq.reshape(Q_LORA, MLA_HEADS, QK_NOPE + QK_ROPE)
    wuq = jnp.concatenate([wq, _swap_halves(wq[..., QK_NOPE:])], axis=-1)
    wuq = wuq.reshape(Q_LORA, MLA_HEADS * QK_PAD).astype(BF16)
    q, k, v = mla_prep(x, gain.reshape(1, d), win, q_gain.reshape(1, Q_LORA), kv_gain.reshape(1, KV_LORA),
                       wuq, w_ukv.astype(BF16), cs)
    o = flash_attention(q, k, v)
    return matmul_residual(o.reshape(b * s, MLA_HEADS * V_DIM), w_out.astype(BF16),
                           x.reshape(b * s, d)).reshape(b, s, d)


def _peer_layer(x, gain, w_q, keys, u_tab, v_tab):
    b, s, d = x.shape
    x2 = x.reshape(b * s, d)
    keys16 = keys.reshape(PEER_HEADS * 2, N_KEYS, PEER_HALF).astype(BF16)
    xn, e1, e2, tau = peer_router(x2, gain, w_q.astype(BF16), keys16)
    out = peer_experts(xn, x2, e1, e2, tau, u_tab.astype(BF16), v_tab.astype(BF16).T)
    return out.reshape(b, s, d)


def kernel(x, norm_mix, norm_ffn, norm_final, ev_w_in, ev_gm_v_gain, ev_gm_ws, ev_gm_b, ev_conv_w, ev_conv_b, ev_i_bias, ev_f_bias, ev_h_gain, ev_w_out, od_w_in, od_q_gain, od_kv_gain, od_w_uq, od_w_ukv, od_w_out, peer_w_q, peer_keys, peer_u, peer_v):
    b, s, d = x.shape
    depth = norm_mix.shape[0]
    pos = jnp.arange(s, dtype=F32)
    inv_freq = ROPE_THETA ** (-jnp.arange(QK_ROPE // 2, dtype=F32) / (QK_ROPE // 2))
    ang = pos[:, None] * inv_freq[None, :]
    cos, sin = jnp.cos(ang), jnp.sin(ang)
    cs = jnp.concatenate([cos, cos, -sin, sin], axis=1)
    for layer in range(depth):
        j = layer // 2
        if layer % 2 == 0:
            x = _even_layer(x, norm_mix[layer], ev_w_in[j], ev_gm_v_gain[j], ev_gm_ws[j], ev_gm_b[j],
                            ev_conv_w[j], ev_conv_b[j], ev_i_bias[j], ev_f_bias[j], ev_h_gain[j], ev_w_out[j])
        else:
            x = _odd_layer(x, norm_mix[layer], od_w_in[j], od_q_gain[j], od_kv_gain[j], od_w_uq[j],
                           od_w_ukv[j], od_w_out[j], cs)
        x = _peer_layer(x, norm_ffn[layer], peer_w_q[layer], peer_keys[layer], peer_u[layer], peer_v[layer])
    return final_norm(x.reshape(b * s, d), norm_final).reshape(b, s, d)
```

```python
import functools
import math

import jax
import jax.numpy as jnp
import numpy as np
from jax import lax
from jax.experimental import pallas as pl
from jax.experimental.pallas import tpu as pltpu

F32 = jnp.float32
BF16 = jnp.bfloat16
EPS = 1e-6
LANES = 128
GELU_C1 = math.sqrt(2.0 / math.pi)
GELU_C3 = 0.044715 * GELU_C1
VMEM_LIMIT = 56 * 1024 * 1024

GM_GROUPS = 4
GM_W = 512
ML_HEADS = 4
ML_W = 512
CHUNK = 128
CONV_WIDTH = 4
EV_IN = 2 * GM_W + 4 * ML_W + 2 * ML_HEADS
EV_IN_PAD = 2 * GM_W + 4 * ML_W + LANES
MLA_HEADS = 8
QK_NOPE = 128
QK_ROPE = 64
V_DIM = 128
Q_LORA = 512
KV_LORA = 256
ROPE_THETA = 10000.0
QK_PAD = 256
V_PAD = 256
PEER_HEADS = 8
N_KEYS = 128
PEER_TOPK = 16
PEER_HALF = 128


def _params(*sem):
    return pltpu.CompilerParams(dimension_semantics=sem, vmem_limit_bytes=VMEM_LIMIT)


def _rms(x, gain):
    return x * lax.rsqrt(jnp.mean(x * x, axis=-1, keepdims=True) + EPS) * gain


def _gelu(x):
    c = math.sqrt(2.0 / math.pi)
    return x * (0.5 * (1.0 + jnp.tanh(c * (x + 0.044715 * (x * x * x)))))


def _sigmoid(x):
    return 1.0 / (1.0 + jnp.exp(-x))


def _dot(a, b):
    return jnp.dot(a, b, preferred_element_type=F32)


def _dot_nt(a, b):
    return lax.dot_general(a, b, (((1,), (1,)), ((), ())), preferred_element_type=F32)


def _dot_tn(a, b):
    return lax.dot_general(a, b, (((0,), (0,)), ((), ())), preferred_element_type=F32)


def _norm_matmul_kernel(x_ref, g_ref, w_ref, o_ref):
    xn = _rms(x_ref[...], g_ref[...])
    o_ref[...] = _dot(xn.astype(BF16), w_ref[...])


def norm_matmul(x, gain, w, tm=512):
    n, d = x.shape
    m = w.shape[1]
    tm = min(tm, n)
    return pl.pallas_call(
        _norm_matmul_kernel,
        grid=(n // tm,),
        in_specs=[pl.BlockSpec((tm, d), lambda i: (i, 0)),
                  pl.BlockSpec((1, d), lambda i: (0, 0)),
                  pl.BlockSpec((d, m), lambda i: (0, 0))],
        out_specs=pl.BlockSpec((tm, m), lambda i: (i, 0)),
        out_shape=jax.ShapeDtypeStruct((n, m), F32),
        compiler_params=_params("parallel"),
        name="norm_matmul",
    )(x, gain.reshape(1, d), w)


def _matmul_residual_kernel(a_ref, w_ref, x_ref, o_ref):
    o_ref[...] = x_ref[...] + _dot(a_ref[...], w_ref[...])


def matmul_residual(a, w, x, tm=1024):
    n, k = a.shape
    d = w.shape[1]
    tm = min(tm, n)
    return pl.pallas_call(
        _matmul_residual_kernel,
        grid=(n // tm,),
        in_specs=[pl.BlockSpec((tm, k), lambda i: (i, 0)),
                  pl.BlockSpec((k, d), lambda i: (0, 0)),
                  pl.BlockSpec((tm, d), lambda i: (i, 0))],
        out_specs=pl.BlockSpec((tm, d), lambda i: (i, 0)),
        out_shape=jax.ShapeDtypeStruct((n, d), F32),
        compiler_params=_params("parallel"),
        name="matmul_residual",
    )(a, w, x)


def _even_mixer_kernel(proj_ref, x_ref, vgain_ref, ws_ref, sbt_ref, convw_ref, convb_ref,
                       gbias_ref, hgain_ref, wout_ref, o_ref,
                       zbuf, cext, mstate, mixed, *, n_chunks):
    @pl.when(pl.program_id(1) == 0)
    def _():
        zbuf[0:8, :] = jnp.zeros((8, 2 * ML_W), F32)
        cext[...] = jnp.zeros(cext.shape, F32)
        mstate[...] = jnp.zeros(mstate.shape, F32)

    row = lax.broadcasted_iota(jnp.int32, (CHUNK, CHUNK), 0)
    col = lax.broadcasted_iota(jnp.int32, (CHUNK, CHUNK), 1)
    causal = col <= row
    tri = jnp.where(causal, 1.0, 0.0).astype(F32)
    ones_col = jnp.where(col == 0, 1.0, 0.0).astype(F32)
    q_scale = float(CHUNK) ** -0.5

    for c in range(n_chunks):
        rows = slice(c * CHUNK, (c + 1) * CHUNK)

        for g in range(GM_GROUPS):
            lanes = slice(g * LANES, (g + 1) * LANES)
            u = _gelu(proj_ref[0, rows, lanes])
            v = _gelu(proj_ref[0, rows, GM_W + g * LANES:GM_W + (g + 1) * LANES])
            vn = _rms(v, vgain_ref[:, lanes])
            wm = jnp.where(causal, ws_ref[g], 0.0)
            s = _dot(wm.astype(BF16), vn.astype(BF16)) + sbt_ref[:, g:g + 1]
            mixed[rows, lanes] = (u * s).astype(BF16)

        zbuf[8:8 + CHUNK, :] = proj_ref[0, rows, 2 * GM_W:2 * GM_W + 2 * ML_W]
        conv = convb_ref[...]
        for j in range(CONV_WIDTH):
            lo = 8 - (CONV_WIDTH - 1) + j
            conv = conv + convw_ref[j:j + 1, :] * zbuf[lo:lo + CHUNK, :]
        zbuf[0:8, :] = zbuf[CHUNK:CHUNK + 8, :]
        qk = conv * _sigmoid(conv)

        gt = proj_ref[0, rows, EV_IN_PAD - LANES:EV_IN_PAD] + gbias_ref[...]
        logf = jnp.minimum(gt, 0.0) - jnp.log1p(jnp.exp(-jnp.abs(gt)))
        bcum = jnp.dot(tri, logf, preferred_element_type=F32, precision=lax.Precision.HIGHEST)
        gt_t = gt.T
        bcum_t = bcum.T

        for hd in range(ML_HEADS):
            lanes = slice(hd * LANES, (hd + 1) * LANES)
            q = (qk[:, lanes] * q_scale).astype(BF16)
            kf = qk[:, ML_W + hd * LANES:ML_W + (hd + 1) * LANES]
            k = kf.astype(BF16)
            v = proj_ref[0, rows, 2 * GM_W + 2 * ML_W + hd * LANES:2 * GM_W + 2 * ML_W + (hd + 1) * LANES]
            o_pre = proj_ref[0, rows, 2 * GM_W + 3 * ML_W + hd * LANES:2 * GM_W + 3 * ML_W + (hd + 1) * LANES]
            b_col = bcum[:, ML_HEADS + hd:ML_HEADS + hd + 1]
            b_row = bcum_t[ML_HEADS + hd:ML_HEADS + hd + 1, :]
            i_col = gt[:, hd:hd + 1]
            i_row = gt_t[hd:hd + 1, :]
            m_prev = mstate[hd][:, 0:1]
            c_prev = cext[hd]

            dmat = jnp.where(causal, b_col - b_row + i_row, -jnp.inf)
            inter = b_col + m_prev
            m_t = jnp.maximum(inter, jnp.max(dmat, axis=-1, keepdims=True))
            w_intra = jnp.exp(dmat - m_t)
            w_inter = jnp.exp(inter - m_t)
            sm = _dot_nt(q, k) * w_intra
            v_ext = jnp.concatenate([v, ones_col], axis=1)
            numden = _dot(sm.astype(BF16), v_ext.astype(BF16)) + w_inter * _dot(q, c_prev.astype(BF16))
            num = numden[:, :LANES]
            den = numden[:, LANES:LANES + 1]
            h = num / jnp.maximum(jnp.abs(den), jnp.exp(-m_t))

            b_last = bcum[CHUNK - 1:CHUNK, ML_HEADS + hd:ML_HEADS + hd + 1]
            gdec = b_last - b_col + i_col
            m_new = jnp.maximum(b_last + m_prev, jnp.max(gdec, axis=0, keepdims=True))
            decay = jnp.exp(b_last + m_prev - m_new)
            w_new = jnp.exp(gdec - m_new)
            cext[hd] = decay * c_prev + _dot(kf.T.astype(BF16), (w_new * v_ext).astype(BF16))
            mstate[hd] = jnp.broadcast_to(m_new, (1, LANES))

            hn = _rms(h, hgain_ref[:, lanes])
            mixed[rows, GM_W + hd * LANES:GM_W + (hd + 1) * LANES] = (_sigmoid(o_pre) * hn).astype(BF16)

    o_ref[0] = x_ref[0] + _dot(mixed[...], wout_ref[...])


def even_mixer(proj, x, vgain, ws, sbt, convw, convb, gbias, hgain, wout, tt=512):
    b, s, d = x.shape
    tt = min(tt, s)
    const = lambda shape: pl.BlockSpec(shape, lambda bi, ti: (0,) * len(shape))
    return pl.pallas_call(
        functools.partial(_even_mixer_kernel, n_chunks=tt // CHUNK),
        grid=(b, s // tt),
        in_specs=[pl.BlockSpec((1, tt, EV_IN_PAD), lambda bi, ti: (bi, ti, 0)),
                  pl.BlockSpec((1, tt, d), lambda bi, ti: (bi, ti, 0)),
                  const((1, GM_W)), const((GM_GROUPS, CHUNK, CHUNK)), const((CHUNK, GM_GROUPS)),
                  const((CONV_WIDTH, 2 * ML_W)), const((1, 2 * ML_W)), const((1, LANES)),
                  const((1, ML_W)), const((GM_W + ML_W, d))],
        out_specs=pl.BlockSpec((1, tt, d), lambda bi, ti: (bi, ti, 0)),
        out_shape=jax.ShapeDtypeStruct((b, s, d), F32),
        scratch_shapes=[pltpu.VMEM((CHUNK + 8, 2 * ML_W), F32),
                        pltpu.VMEM((ML_HEADS, CHUNK, 2 * LANES), F32),
                        pltpu.VMEM((ML_HEADS, 1, LANES), F32),
                        pltpu.VMEM((tt, GM_W + ML_W), BF16)],
        compiler_params=_params("arbitrary", "arbitrary"),
        name="even_mixer",
    )(proj, x, vgain, ws, sbt, convw, convb, gbias, hgain, wout)


def _mla_prep_kernel(x_ref, g_ref, win_ref, qg_ref, kvg_ref, wuq_ref, wukv_ref, cs_ref,
                     q_ref, k_ref, v_ref):
    xn = _rms(x_ref[0], g_ref[...])
    proj = _dot(xn.astype(BF16), win_ref[...])
    c_q = _rms(proj[:, :Q_LORA], qg_ref[...])
    c_kv = _rms(proj[:, Q_LORA:Q_LORA + KV_LORA], kvg_ref[...])
    cs = cs_ref[...]
    tm = cs.shape[0]
    lane = lax.broadcasted_iota(jnp.int32, (tm, LANES), 1)
    t = proj[:, Q_LORA + KV_LORA:] * cs
    k_rope_t = jnp.where(lane < QK_ROPE, t + pltpu.roll(t, QK_ROPE, axis=1), 0.0).T.astype(BF16)
    q_all = _dot(c_q.astype(BF16), wuq_ref[...])
    kv_all = _dot(c_kv.astype(BF16), wukv_ref[...])
    scale = float(QK_NOPE + QK_ROPE) ** -0.5
    ones_col = jnp.where(lane == 0, 1.0, 0.0).astype(BF16)
    for h in range(MLA_HEADS):
        o = h * QK_PAD
        qt = q_all[:, o + QK_NOPE:o + QK_PAD] * cs
        q_ref[0, h, :, 0:QK_NOPE] = (q_all[:, o:o + QK_NOPE] * scale).astype(BF16)
        q_ref[0, h, :, QK_NOPE:QK_PAD] = ((qt + pltpu.roll(qt, QK_ROPE, axis=1)) * scale).astype(BF16)
        k_ref[0, h, 0:QK_NOPE, :] = kv_all[:, o:o + QK_NOPE].T.astype(BF16)
        k_ref[0, h, QK_NOPE:QK_PAD, :] = k_rope_t
        v_ref[0, h, :, 0:V_DIM] = kv_all[:, o + QK_NOPE:o + QK_PAD].astype(BF16)
        v_ref[0, h, :, V_DIM:V_PAD] = ones_col


def mla_prep(x, gain, win, qg, kvg, wuq, wukv, cs, tm=512):
    b, s, d = x.shape
    tm = min(tm, s)
    const = lambda shape: pl.BlockSpec(shape, lambda bi, ti: (0,) * len(shape))
    head_out = lambda w: pl.BlockSpec((1, MLA_HEADS, tm, w), lambda bi, ti: (bi, 0, ti, 0))
    return pl.pallas_call(
        _mla_prep_kernel,
        grid=(b, s // tm),
        in_specs=[pl.BlockSpec((1, tm, d), lambda bi, ti: (bi, ti, 0)),
                  const((1, d)), const(win.shape), const((1, Q_LORA)), const((1, KV_LORA)),
                  const(wuq.shape), const(wukv.shape),
                  pl.BlockSpec((tm, LANES), lambda bi, ti: (ti, 0))],
        out_specs=[head_out(QK_PAD),
                   pl.BlockSpec((1, MLA_HEADS, QK_PAD, tm), lambda bi, ti: (bi, 0, 0, ti)),
                   head_out(V_PAD)],
        out_shape=[jax.ShapeDtypeStruct((b, MLA_HEADS, s, QK_PAD), BF16),
                   jax.ShapeDtypeStruct((b, MLA_HEADS, QK_PAD, s), BF16),
                   jax.ShapeDtypeStruct((b, MLA_HEADS, s, V_PAD), BF16)],
        compiler_params=_params("parallel", "parallel"),
        name="mla_prep",
    )(x, gain, win, qg, kvg, wuq, wukv, cs)


def _flash_kernel(qi_ref, ki_ref, q_ref, k_ref, v_ref, o_ref, m_sc, acc_sc, *, blk):
    step = pl.program_id(1)
    qi = qi_ref[step]
    ki = ki_ref[step]

    @pl.when(ki == 0)
    def _():
        m_sc[...] = jnp.full(m_sc.shape, -jnp.inf, F32)
        acc_sc[...] = jnp.zeros(acc_sc.shape, F32)

    def sweep(diagonal):
        def scores(h):
            return _dot(q_ref[0, h], k_ref[0, h])

        if diagonal:
            r = lax.broadcasted_iota(jnp.int32, (blk, blk), 0)
            c = lax.broadcasted_iota(jnp.int32, (blk, blk), 1)
            causal_bias = jnp.where(c <= r, 0.0, -jnp.inf)
        s_next = scores(0)
        for h in range(MLA_HEADS):
            s = s_next
            if h + 1 < MLA_HEADS:
                s_next = scores(h + 1)
            if diagonal:
                s = s + causal_bias
            m_prev = m_sc[h]
            m_new = jnp.maximum(m_prev, jnp.max(s, axis=1, keepdims=True))
            alpha = jnp.exp(m_prev - m_new)
            p = jnp.exp((s - m_new[:, 0:1]).astype(BF16))
            acc_sc[h] = jnp.concatenate([alpha, alpha], axis=1) * acc_sc[h] + _dot(p, v_ref[0, h])
            m_sc[h] = m_new

    @pl.when(ki < qi)
    def _():
        sweep(False)

    @pl.when(ki == qi)
    def _():
        sweep(True)
        for h in range(MLA_HEADS):
            acc = acc_sc[h]
            o_ref[0, :, h * V_DIM:(h + 1) * V_DIM] = (acc[:, :V_DIM] / acc[:, V_DIM:V_DIM + 1]).astype(BF16)


def flash_attention(q, k, v, blk=512):
    b, nh, s, _ = q.shape
    blk = min(blk, s)
    nq = s // blk
    qi = np.concatenate([np.full(i + 1, i, np.int32) for i in range(nq)])
    ki = np.concatenate([np.arange(i + 1, dtype=np.int32) for i in range(nq)])
    grid_spec = pltpu.PrefetchScalarGridSpec(
        num_scalar_prefetch=2,
        grid=(b, len(qi)),
        in_specs=[pl.BlockSpec((1, nh, blk, QK_PAD), lambda bi, p, qi_r, ki_r: (bi, 0, qi_r[p], 0)),
                  pl.BlockSpec((1, nh, QK_PAD, blk), lambda bi, p, qi_r, ki_r: (bi, 0, 0, ki_r[p])),
                  pl.BlockSpec((1, nh, blk, V_PAD), lambda bi, p, qi_r, ki_r: (bi, 0, ki_r[p], 0))],
        out_specs=pl.BlockSpec((1, blk, nh * V_DIM), lambda bi, p, qi_r, ki_r: (bi, qi_r[p], 0)),
        scratch_shapes=[pltpu.VMEM((nh, blk, LANES), F32),
                        pltpu.VMEM((nh, blk, V_PAD), F32)],
    )
    return pl.pallas_call(
        functools.partial(_flash_kernel, blk=blk),
        grid_spec=grid_spec,
        out_shape=jax.ShapeDtypeStruct((b, s, nh * V_DIM), BF16),
        compiler_params=_params("arbitrary", "arbitrary"),
        name="mla_attention",
    )(jnp.asarray(qi), jnp.asarray(ki), q, k, v)


def _sorting_network(n):
    pairs = []
    p = 1
    while p < n:
        k = p
        while k >= 1:
            for j in range(k % p, n - k, 2 * k):
                for i in range(min(k, n - j - k)):
                    if (i + j) // (2 * p) == (i + j + k) // (2 * p):
                        pairs.append((i + j, i + j + k))
            k //= 2
        p *= 2
    return pairs


def _largest16(tiles, only_last):
    n = PEER_TOPK
    v = list(tiles)

    def exchange(i, j):
        v[i], v[j] = jnp.maximum(v[i], v[j]), jnp.minimum(v[i], v[j])

    for i, j in _sorting_network(n):
        exchange(i, j)
    shift = v[0].shape[0] // 2
    while shift >= 1:
        other = [pltpu.roll(x, shift, axis=0) for x in v]
        v = [jnp.maximum(v[i], other[n - 1 - i]) for i in range(n)]
        if only_last and shift == 1:
            return functools.reduce(jnp.minimum, v)
        stride = n // 2
        while stride >= 1:
            for i in range(n):
                if (i // stride) % 2 == 0:
                    exchange(i, i + stride)
            stride //= 2
        shift //= 2
    return v


def _top16(e):
    sub_rows = e.shape[0] // PEER_TOPK
    v = _largest16([e[r * sub_rows:(r + 1) * sub_rows] for r in range(PEER_TOPK)], only_last=False)
    sub = lax.broadcasted_iota(jnp.int32, v[0].shape, 0)
    halves = []
    for base in range(0, PEER_TOPK, sub_rows):
        rows = v[base]
        for r in range(1, sub_rows):
            rows = jnp.where(sub == r, v[base + r], rows)
        halves.append(rows)
    return jnp.concatenate(halves, axis=0)


def _pair_candidates(ea, eb):
    parts = [ea[0:1] * eb]
    parts += [ea[p:p + 1] * eb[0:8] for p in range(1, 8)]
    parts += [ea[8:16] * eb[0:1]]
    return jnp.concatenate(parts, axis=0)


def _router_kernel(x_ref, g_ref, wq_ref, keys_ref, xn_ref, e1_ref, e2_ref, tau_ref):
    xn_f32 = _rms(x_ref[...], g_ref[...])
    xn_ref[...] = xn_f32.T.astype(BF16)
    q = _dot(xn_f32.astype(BF16), wq_ref[...]).astype(BF16)
    for h in range(PEER_HEADS):
        es, tops = [], []
        for half in range(2):
            j = 2 * h + half
            s_t = _dot_nt(keys_ref[j], q[:, j * PEER_HALF:(j + 1) * PEER_HALF])
            e = jnp.exp(s_t - jnp.max(s_t, axis=0, keepdims=True))
            es.append(e)
            tops.append(_top16(e))
        ea, eb = tops
        cand = _pair_candidates(ea, eb)
        n_tiles = cand.shape[0] // 8
        tiles = [cand[8 * r:8 * (r + 1)] for r in range(n_tiles)]
        tiles += [jnp.zeros_like(tiles[0])] * (PEER_TOPK - n_tiles)
        kth = _largest16(tiles, only_last=True)[0:1]
        sel = cand >= jnp.maximum(kth, 1e-30)
        z = jnp.sum(jnp.where(sel, cand, 0.0), axis=0, keepdims=True)
        scale = 0.5 / z
        cand_z = _pair_candidates(ea * scale, eb)
        tau_ref[h:h + 1, :] = jnp.min(jnp.where(sel, cand_z, jnp.inf), axis=0, keepdims=True)
        e1_ref[h] = es[0] * scale
        e2_ref[h] = es[1]


def peer_router(x, gain, wq, keys, tr=256):
    n, d = x.shape
    tr = min(tr, n)
    return pl.pallas_call(
        _router_kernel,
        grid=(n // tr,),
        in_specs=[pl.BlockSpec((tr, d), lambda i: (i, 0)),
                  pl.BlockSpec((1, d), lambda i: (0, 0)),
                  pl.BlockSpec(wq.shape, lambda i: (0, 0)),
                  pl.BlockSpec(keys.shape, lambda i: (0, 0, 0))],
        out_specs=[pl.BlockSpec((d, tr), lambda i: (0, i)),
                   pl.BlockSpec((PEER_HEADS, N_KEYS, tr), lambda i: (0, 0, i)),
                   pl.BlockSpec((PEER_HEADS, N_KEYS, tr), lambda i: (0, 0, i)),
                   pl.BlockSpec((PEER_HEADS, tr), lambda i: (0, i))],
        out_shape=[jax.ShapeDtypeStruct((d, n), BF16),
                   jax.ShapeDtypeStruct((PEER_HEADS, N_KEYS, n), F32),
                   jax.ShapeDtypeStruct((PEER_HEADS, N_KEYS, n), F32),
                   jax.ShapeDtypeStruct((PEER_HEADS, n), F32)],
        compiler_params=_params("parallel"),
        name="peer_router",
    )(x, gain.reshape(1, d), wq, keys)


def _peer_expert_kernel(xn_ref, x_ref, e1_ref, e2_ref, tau_ref, u_ref, v_ref, out_gain_ref, o_ref,
                        acc_sc, act_sc, w_sc, *, et, n_sub, tc, norm_output):
    e = pl.program_id(1)
    tb = xn_ref.shape[1]
    n_i = et // N_KEYS

    @pl.when(e == 0)
    def _():
        acc_sc[...] = jnp.zeros(acc_sc.shape, F32)

    mxu_w = 2 * LANES

    def rows(s):
        return slice(s * et, (s + 1) * et)

    def act_pieces(s):
        def piece(c):
            cols = slice(c * mxu_w, (c + 1) * mxu_w)
            act_sc[s, :, cols] = _dot(u_ref[rows(s), :], xn_ref[:, cols])
        return [functools.partial(piece, c) for c in range(tb // mxu_w)]

    def out_pieces(s):
        def piece(c):
            cols = slice(c * mxu_w, (c + 1) * mxu_w)
            acc_sc[:, cols] += _dot(v_ref[:, rows(s)], w_sc[s, :, cols])
        return [functools.partial(piece, c) for c in range(tb // mxu_w)]

    def build_pieces(s):
        def piece(i, c):
            ig = (e * n_sub + s) * n_i + i
            r = slice(i * N_KEYS, (i + 1) * N_KEYS)
            cols = slice(c * tc, (c + 1) * tc)

            gates = jnp.zeros((N_KEYS, tc), F32)
            for h in range(PEER_HEADS):
                p = e2_ref[h, :, cols] * e1_ref[h, pl.ds(ig, 1), :][:, cols]
                gates = gates + jnp.where(p >= tau_ref[h:h + 1, cols], p, 0.0)
            a = act_sc[s, r, cols]
            two_gelu = a * (1.0 + jnp.tanh(a * (GELU_C1 + GELU_C3 * (a * a))))
            w_sc[s, r, cols] = (gates * two_gelu).astype(BF16)
        return [functools.partial(piece, i, c) for i in range(n_i) for c in range(tb // tc)]

    def interleave(vector_work, matrix_work):
        done = 0
        for k, piece in enumerate(vector_work):
            piece()
            due = (k + 1) * len(matrix_work) // len(vector_work)
            for m in matrix_work[done:due]:
                m()
            done = due

    for m in act_pieces(0):
        m()
    for s in range(n_sub):
        matrix_work = (act_pieces(s + 1) if s + 1 < n_sub else []) + (out_pieces(s - 1) if s > 0 else [])
        interleave(build_pieces(s), matrix_work)
    for m in out_pieces(n_sub - 1):
        m()

    @pl.when(e == pl.num_programs(1) - 1)
    def _():
        y = x_ref[...] + acc_sc[...].T
        o_ref[...] = _rms(y, out_gain_ref[...]) if norm_output else y


def peer_experts(xn, x, e1, e2, tau, u, v, out_gain, norm_output, tb=512, et=512, n_sub=4, tc=128):
    n, d = x.shape
    ne = u.shape[0]
    tb = min(tb, n)
    tc = min(tc, tb)
    blk_e = et * n_sub
    per_key = pl.BlockSpec((PEER_HEADS, N_KEYS, tb), lambda i, j: (0, 0, i))
    return pl.pallas_call(
        functools.partial(_peer_expert_kernel, et=et, n_sub=n_sub, tc=tc, norm_output=norm_output),
        grid=(n // tb, ne // blk_e),
        in_specs=[pl.BlockSpec((d, tb), lambda i, j: (0, i)),
                  pl.BlockSpec((tb, d), lambda i, j: (i, 0)),
                  per_key, per_key, pl.BlockSpec((PEER_HEADS, tb), lambda i, j: (0, i)),
                  pl.BlockSpec((blk_e, d), lambda i, j: (j, 0)),
                  pl.BlockSpec((d, blk_e), lambda i, j: (0, j)),
                  pl.BlockSpec((1, d), lambda i, j: (0, 0))],
        out_specs=pl.BlockSpec((tb, d), lambda i, j: (i, 0)),
        out_shape=jax.ShapeDtypeStruct((n, d), F32),
        scratch_shapes=[pltpu.VMEM((d, tb), F32),
                        pltpu.VMEM((n_sub, et, tb), F32),
                        pltpu.VMEM((n_sub, et, tb), BF16)],
        compiler_params=_params("parallel", "arbitrary"),
        name="peer_experts",
    )(xn, x, e1, e2, tau, u, v, out_gain.reshape(1, d))


def _final_norm_kernel(x_ref, g_ref, o_ref):
    o_ref[...] = _rms(x_ref[...], g_ref[...])


def final_norm(x, gain, tm=1024):
    n, d = x.shape
    tm = min(tm, n)
    return pl.pallas_call(
        _final_norm_kernel,
        grid=(n // tm,),
        in_specs=[pl.BlockSpec((tm, d), lambda i: (i, 0)), pl.BlockSpec((1, d), lambda i: (0, 0))],
        out_specs=pl.BlockSpec((tm, d), lambda i: (i, 0)),
        out_shape=jax.ShapeDtypeStruct((n, d), F32),
        compiler_params=_params("parallel"),
        name="final_norm",
    )(x, gain.reshape(1, d))


def _swap_halves(w):
    half = w.shape[-1] // 2
    return jnp.concatenate([w[..., half:], w[..., :half]], axis=-1)


def _even_layer(x, gain, w_in, v_gain, ws, sb, conv_w, conv_b, i_bias, f_bias, h_gain, w_out):
    b, s, d = x.shape
    w_in_p = jnp.pad(w_in, ((0, 0), (0, EV_IN_PAD - EV_IN))).astype(BF16)
    proj = norm_matmul(x.reshape(b * s, d), gain, w_in_p).reshape(b, s, EV_IN_PAD)
    gbias = jnp.pad(jnp.concatenate([i_bias, f_bias]), (0, LANES - 2 * ML_HEADS)).reshape(1, LANES)
    return even_mixer(proj, x, v_gain.reshape(1, GM_W), ws, sb.T, conv_w, conv_b.reshape(1, 2 * ML_W),
                      gbias, h_gain.reshape(1, ML_W), w_out.astype(BF16))


def _odd_layer(x, gain, w_in, q_gain, kv_gain, w_uq, w_ukv, w_out, cs):
    b, s, d = x.shape
    rope_cols = w_in[:, Q_LORA + KV_LORA:]
    win = jnp.concatenate([w_in, _swap_halves(rope_cols)], axis=1).astype(BF16)
    wq = w_uq.reshape(Q_LORA, MLA_HEADS, QK_NOPE + QK_ROPE)
    wuq = jnp.concatenate([wq, _swap_halves(wq[..., QK_NOPE:])], axis=-1)
    wuq = wuq.reshape(Q_LORA, MLA_HEADS * QK_PAD).astype(BF16)
    q, k, v = mla_prep(x, gain.reshape(1, d), win, q_gain.reshape(1, Q_LORA), kv_gain.reshape(1, KV_LORA),
                       wuq, w_ukv.astype(BF16), cs)
    o = flash_attention(q, k, v)
    return matmul_residual(o.reshape(b * s, MLA_HEADS * V_DIM), w_out.astype(BF16),
                           x.reshape(b * s, d)).reshape(b, s, d)


def _peer_layer(x, gain, w_q, keys, u_tab, v_tab, out_gain, norm_output):
    b, s, d = x.shape
    x2 = x.reshape(b * s, d)
    keys16 = keys.reshape(PEER_HEADS * 2, N_KEYS, PEER_HALF).astype(BF16)
    xn, e1, e2, tau = peer_router(x2, gain, w_q.astype(BF16), keys16)
    out = peer_experts(xn, x2, e1, e2, tau, u_tab.astype(BF16), v_tab.astype(BF16).T, out_gain, norm_output)
    return out.reshape(b, s, d)


def kernel(x, norm_mix, norm_ffn, norm_final, ev_w_in, ev_gm_v_gain, ev_gm_ws, ev_gm_b, ev_conv_w, ev_conv_b, ev_i_bias, ev_f_bias, ev_h_gain, ev_w_out, od_w_in, od_q_gain, od_kv_gain, od_w_uq, od_w_ukv, od_w_out, peer_w_q, peer_keys, peer_u, peer_v):
    b, s, d = x.shape
    depth = norm_mix.shape[0]
    pos = jnp.arange(s, dtype=F32)
    inv_freq = ROPE_THETA ** (-jnp.arange(QK_ROPE // 2, dtype=F32) / (QK_ROPE // 2))
    ang = pos[:, None] * inv_freq[None, :]
    cos, sin = jnp.cos(ang), jnp.sin(ang)
    cs = jnp.concatenate([cos, cos, -sin, sin], axis=1)
    for layer in range(depth):
        j = layer // 2
        if layer % 2 == 0:
            x = _even_layer(x, norm_mix[layer], ev_w_in[j], ev_gm_v_gain[j], ev_gm_ws[j], ev_gm_b[j],
                            ev_conv_w[j], ev_conv_b[j], ev_i_bias[j], ev_f_bias[j], ev_h_gain[j], ev_w_out[j])
        else:
            x = _odd_layer(x, norm_mix[layer], od_w_in[j], od_q_gain[j], od_kv_gain[j], od_w_uq[j],
                           od_w_ukv[j], od_w_out[j], cs)
        x = _peer_layer(x, norm_ffn[layer], peer_w_q[layer], peer_keys[layer], peer_u[layer], peer_v[layer],
                        norm_final, norm_output=(layer == depth - 1))
    return x
```

```python
import functools
import math

import jax
import jax.numpy as jnp
import numpy as np
from jax import lax
from jax.experimental import pallas as pl
from jax.experimental.pallas import tpu as pltpu

F32 = jnp.float32
BF16 = jnp.bfloat16
EPS = 1e-6
LANES = 128
GELU_C1 = math.sqrt(2.0 / math.pi)
GELU_C3 = 0.044715 * GELU_C1
VMEM_LIMIT = 56 * 1024 * 1024

GM_GROUPS = 4
GM_W = 512
ML_HEADS = 4
ML_W = 512
CHUNK = 128
CONV_WIDTH = 4
EV_IN = 2 * GM_W + 4 * ML_W + 2 * ML_HEADS
EV_IN_PAD = 2 * GM_W + 4 * ML_W + LANES
MLA_HEADS = 8
QK_NOPE = 128
QK_ROPE = 64
V_DIM = 128
Q_LORA = 512
KV_LORA = 256
ROPE_THETA = 10000.0
QK_PAD = 256
V_PAD = 256
PEER_HEADS = 8
N_KEYS = 128
PEER_TOPK = 16
PEER_HALF = 128


def _params(*sem):
    return pltpu.CompilerParams(dimension_semantics=sem, vmem_limit_bytes=VMEM_LIMIT)


def _rms(x, gain):
    return x * lax.rsqrt(jnp.mean(x * x, axis=-1, keepdims=True) + EPS) * gain


def _gelu(x):
    c = math.sqrt(2.0 / math.pi)
    return x * (0.5 * (1.0 + jnp.tanh(c * (x + 0.044715 * (x * x * x)))))


def _sigmoid(x):
    return 1.0 / (1.0 + jnp.exp(-x))


def _dot(a, b):
    return jnp.dot(a, b, preferred_element_type=F32)


def _dot_nt(a, b):
    return lax.dot_general(a, b, (((1,), (1,)), ((), ())), preferred_element_type=F32)


def _dot_tn(a, b):
    return lax.dot_general(a, b, (((0,), (0,)), ((), ())), preferred_element_type=F32)


def _norm_matmul_kernel(x_ref, g_ref, w_ref, o_ref):
    xn = _rms(x_ref[...], g_ref[...])
    o_ref[...] = _dot(xn.astype(BF16), w_ref[...])


def norm_matmul(x, gain, w, tm=512):
    n, d = x.shape
    m = w.shape[1]
    tm = min(tm, n)
    return pl.pallas_call(
        _norm_matmul_kernel,
        grid=(n // tm,),
        in_specs=[pl.BlockSpec((tm, d), lambda i: (i, 0)),
                  pl.BlockSpec((1, d), lambda i: (0, 0)),
                  pl.BlockSpec((d, m), lambda i: (0, 0))],
        out_specs=pl.BlockSpec((tm, m), lambda i: (i, 0)),
        out_shape=jax.ShapeDtypeStruct((n, m), F32),
        compiler_params=_params("parallel"),
        name="norm_matmul",
    )(x, gain.reshape(1, d), w)


def _matmul_residual_kernel(a_ref, w_ref, x_ref, o_ref):
    o_ref[...] = x_ref[...] + _dot(a_ref[...], w_ref[...])


def matmul_residual(a, w, x, tm=1024):
    n, k = a.shape
    d = w.shape[1]
    tm = min(tm, n)
    return pl.pallas_call(
        _matmul_residual_kernel,
        grid=(n // tm,),
        in_specs=[pl.BlockSpec((tm, k), lambda i: (i, 0)),
                  pl.BlockSpec((k, d), lambda i: (0, 0)),
                  pl.BlockSpec((tm, d), lambda i: (i, 0))],
        out_specs=pl.BlockSpec((tm, d), lambda i: (i, 0)),
        out_shape=jax.ShapeDtypeStruct((n, d), F32),
        compiler_params=_params("parallel"),
        name="matmul_residual",
    )(a, w, x)


def _even_mixer_kernel(proj_ref, x_ref, vgain_ref, ws_ref, sbt_ref, convw_ref, convb_ref,
                       gbias_ref, hgain_ref, wout_ref, o_ref,
                       zbuf, cext, mstate, mixed, *, n_chunks):
    @pl.when(pl.program_id(1) == 0)
    def _():
        zbuf[0:8, :] = jnp.zeros((8, 2 * ML_W), F32)
        cext[...] = jnp.zeros(cext.shape, F32)
        mstate[...] = jnp.zeros(mstate.shape, F32)

    row = lax.broadcasted_iota(jnp.int32, (CHUNK, CHUNK), 0)
    col = lax.broadcasted_iota(jnp.int32, (CHUNK, CHUNK), 1)
    causal = col <= row
    tri = jnp.where(causal, 1.0, 0.0).astype(F32)
    ones_col = jnp.where(col == 0, 1.0, 0.0).astype(F32)
    q_scale = float(CHUNK) ** -0.5

    for c in range(n_chunks):
        rows = slice(c * CHUNK, (c + 1) * CHUNK)

        for g in range(GM_GROUPS):
            lanes = slice(g * LANES, (g + 1) * LANES)
            u = _gelu(proj_ref[0, rows, lanes])
            v = _gelu(proj_ref[0, rows, GM_W + g * LANES:GM_W + (g + 1) * LANES])
            vn = _rms(v, vgain_ref[:, lanes])
            wm = jnp.where(causal, ws_ref[g], 0.0)
            s = _dot(wm.astype(BF16), vn.astype(BF16)) + sbt_ref[:, g:g + 1]
            mixed[rows, lanes] = (u * s).astype(BF16)

        zbuf[8:8 + CHUNK, :] = proj_ref[0, rows, 2 * GM_W:2 * GM_W + 2 * ML_W]
        conv = convb_ref[...]
        for j in range(CONV_WIDTH):
            lo = 8 - (CONV_WIDTH - 1) + j
            conv = conv + convw_ref[j:j + 1, :] * zbuf[lo:lo + CHUNK, :]
        zbuf[0:8, :] = zbuf[CHUNK:CHUNK + 8, :]
        qk = conv * _sigmoid(conv)

        gt = proj_ref[0, rows, EV_IN_PAD - LANES:EV_IN_PAD] + gbias_ref[...]
        logf = jnp.minimum(gt, 0.0) - jnp.log1p(jnp.exp(-jnp.abs(gt)))
        bcum = jnp.dot(tri, logf, preferred_element_type=F32, precision=lax.Precision.HIGHEST)
        gt_t = gt.T
        bcum_t = bcum.T

        for hd in range(ML_HEADS):
            lanes = slice(hd * LANES, (hd + 1) * LANES)
            q = (qk[:, lanes] * q_scale).astype(BF16)
            kf = qk[:, ML_W + hd * LANES:ML_W + (hd + 1) * LANES]
            k = kf.astype(BF16)
            v = proj_ref[0, rows, 2 * GM_W + 2 * ML_W + hd * LANES:2 * GM_W + 2 * ML_W + (hd + 1) * LANES]
            o_pre = proj_ref[0, rows, 2 * GM_W + 3 * ML_W + hd * LANES:2 * GM_W + 3 * ML_W + (hd + 1) * LANES]
            b_col = bcum[:, ML_HEADS + hd:ML_HEADS + hd + 1]
            b_row = bcum_t[ML_HEADS + hd:ML_HEADS + hd + 1, :]
            i_col = gt[:, hd:hd + 1]
            i_row = gt_t[hd:hd + 1, :]
            m_prev = mstate[hd][:, 0:1]
            c_prev = cext[hd]

            dmat = jnp.where(causal, b_col - b_row + i_row, -jnp.inf)
            inter = b_col + m_prev
            m_t = jnp.maximum(inter, jnp.max(dmat, axis=-1, keepdims=True))
            w_intra = jnp.exp(dmat - m_t)
            w_inter = jnp.exp(inter - m_t)
            sm = _dot_nt(q, k) * w_intra
            v_ext = jnp.concatenate([v, ones_col], axis=1)
            numden = _dot(sm.astype(BF16), v_ext.astype(BF16)) + w_inter * _dot(q, c_prev.astype(BF16))
            num = numden[:, :LANES]
            den = numden[:, LANES:LANES + 1]
            h = num / jnp.maximum(jnp.abs(den), jnp.exp(-m_t))

            b_last = bcum[CHUNK - 1:CHUNK, ML_HEADS + hd:ML_HEADS + hd + 1]
            gdec = b_last - b_col + i_col
            m_new = jnp.maximum(b_last + m_prev, jnp.max(gdec, axis=0, keepdims=True))
            decay = jnp.exp(b_last + m_prev - m_new)
            w_new = jnp.exp(gdec - m_new)
            cext[hd] = decay * c_prev + _dot(kf.T.astype(BF16), (w_new * v_ext).astype(BF16))
            mstate[hd] = jnp.broadcast_to(m_new, (1, LANES))

            hn = _rms(h, hgain_ref[:, lanes])
            mixed[rows, GM_W + hd * LANES:GM_W + (hd + 1) * LANES] = (_sigmoid(o_pre) * hn).astype(BF16)

    o_ref[0] = x_ref[0] + _dot(mixed[...], wout_ref[...])


def even_mixer(proj, x, vgain, ws, sbt, convw, convb, gbias, hgain, wout, tt=256):
    b, s, d = x.shape
    tt = min(tt, s)
    const = lambda shape: pl.BlockSpec(shape, lambda bi, ti: (0,) * len(shape))
    return pl.pallas_call(
        functools.partial(_even_mixer_kernel, n_chunks=tt // CHUNK),
        grid=(b, s // tt),
        in_specs=[pl.BlockSpec((1, tt, EV_IN_PAD), lambda bi, ti: (bi, ti, 0)),
                  pl.BlockSpec((1, tt, d), lambda bi, ti: (bi, ti, 0)),
                  const((1, GM_W)), const((GM_GROUPS, CHUNK, CHUNK)), const((CHUNK, GM_GROUPS)),
                  const((CONV_WIDTH, 2 * ML_W)), const((1, 2 * ML_W)), const((1, LANES)),
                  const((1, ML_W)), const((GM_W + ML_W, d))],
        out_specs=pl.BlockSpec((1, tt, d), lambda bi, ti: (bi, ti, 0)),
        out_shape=jax.ShapeDtypeStruct((b, s, d), F32),
        scratch_shapes=[pltpu.VMEM((CHUNK + 8, 2 * ML_W), F32),
                        pltpu.VMEM((ML_HEADS, CHUNK, 2 * LANES), F32),
                        pltpu.VMEM((ML_HEADS, 1, LANES), F32),
                        pltpu.VMEM((tt, GM_W + ML_W), BF16)],
        compiler_params=_params("arbitrary", "arbitrary"),
        name="even_mixer",
    )(proj, x, vgain, ws, sbt, convw, convb, gbias, hgain, wout)


def _mla_prep_kernel(x_ref, g_ref, win_ref, qg_ref, kvg_ref, wuq_ref, wukv_ref, cs_ref,
                     q_ref, k_ref, v_ref):
    xn = _rms(x_ref[0], g_ref[...])
    proj = _dot(xn.astype(BF16), win_ref[...])
    c_q = _rms(proj[:, :Q_LORA], qg_ref[...])
    c_kv = _rms(proj[:, Q_LORA:Q_LORA + KV_LORA], kvg_ref[...])
    cs = cs_ref[...]
    tm = cs.shape[0]
    lane = lax.broadcasted_iota(jnp.int32, (tm, LANES), 1)
    t = proj[:, Q_LORA + KV_LORA:] * cs
    k_rope_t = jnp.where(lane < QK_ROPE, t + pltpu.roll(t, QK_ROPE, axis=1), 0.0).T.astype(BF16)
    q_all = _dot(c_q.astype(BF16), wuq_ref[...])
    kv_all = _dot(c_kv.astype(BF16), wukv_ref[...])
    scale = float(QK_NOPE + QK_ROPE) ** -0.5
    ones_col = jnp.where(lane == 0, 1.0, 0.0).astype(BF16)
    for h in range(MLA_HEADS):
        o = h * QK_PAD
        qt = q_all[:, o + QK_NOPE:o + QK_PAD] * cs
        q_ref[0, h, :, 0:QK_NOPE] = (q_all[:, o:o + QK_NOPE] * scale).astype(BF16)
        q_ref[0, h, :, QK_NOPE:QK_PAD] = ((qt + pltpu.roll(qt, QK_ROPE, axis=1)) * scale).astype(BF16)
        k_ref[0, h, 0:QK_NOPE, :] = kv_all[:, o:o + QK_NOPE].T.astype(BF16)
        k_ref[0, h, QK_NOPE:QK_PAD, :] = k_rope_t
        v_ref[0, h, :, 0:V_DIM] = kv_all[:, o + QK_NOPE:o + QK_PAD].astype(BF16)
        v_ref[0, h, :, V_DIM:V_PAD] = ones_col


def mla_prep(x, gain, win, qg, kvg, wuq, wukv, cs, tm=512):
    b, s, d = x.shape
    tm = min(tm, s)
    const = lambda shape: pl.BlockSpec(shape, lambda bi, ti: (0,) * len(shape))
    head_out = lambda w: pl.BlockSpec((1, MLA_HEADS, tm, w), lambda bi, ti: (bi, 0, ti, 0))
    return pl.pallas_call(
        _mla_prep_kernel,
        grid=(b, s // tm),
        in_specs=[pl.BlockSpec((1, tm, d), lambda bi, ti: (bi, ti, 0)),
                  const((1, d)), const(win.shape), const((1, Q_LORA)), const((1, KV_LORA)),
                  const(wuq.shape), const(wukv.shape),
                  pl.BlockSpec((tm, LANES), lambda bi, ti: (ti, 0))],
        out_specs=[head_out(QK_PAD),
                   pl.BlockSpec((1, MLA_HEADS, QK_PAD, tm), lambda bi, ti: (bi, 0, 0, ti)),
                   head_out(V_PAD)],
        out_shape=[jax.ShapeDtypeStruct((b, MLA_HEADS, s, QK_PAD), BF16),
                   jax.ShapeDtypeStruct((b, MLA_HEADS, QK_PAD, s), BF16),
                   jax.ShapeDtypeStruct((b, MLA_HEADS, s, V_PAD), BF16)],
        compiler_params=_params("parallel", "parallel"),
        name="mla_prep",
    )(x, gain, win, qg, kvg, wuq, wukv, cs)


def _flash_kernel(qi_ref, ki_ref, q_ref, k_ref, v_ref, o_ref, m_sc, acc_sc, *, blk):
    step = pl.program_id(1)
    qi = qi_ref[step]
    ki = ki_ref[step]

    @pl.when(ki == 0)
    def _():
        m_sc[...] = jnp.full(m_sc.shape, -jnp.inf, F32)
        acc_sc[...] = jnp.zeros(acc_sc.shape, F32)

    def sweep(diagonal):
        def scores(h):
            return _dot(q_ref[0, h], k_ref[0, h])

        if diagonal:
            r = lax.broadcasted_iota(jnp.int32, (blk, blk), 0)
            c = lax.broadcasted_iota(jnp.int32, (blk, blk), 1)
            causal_bias = jnp.where(c <= r, 0.0, -jnp.inf)
        s_next = scores(0)
        for h in range(MLA_HEADS):
            s = s_next
            if h + 1 < MLA_HEADS:
                s_next = scores(h + 1)
            if diagonal:
                s = s + causal_bias
            m_prev = m_sc[h]
            m_new = jnp.maximum(m_prev, jnp.max(s, axis=1, keepdims=True))
            alpha = jnp.exp(m_prev - m_new)
            p = jnp.exp((s - m_new[:, 0:1]).astype(BF16))
            acc_sc[h] = jnp.concatenate([alpha, alpha], axis=1) * acc_sc[h] + _dot(p, v_ref[0, h])
            m_sc[h] = m_new

    @pl.when(ki < qi)
    def _():
        sweep(False)

    @pl.when(ki == qi)
    def _():
        sweep(True)
        for h in range(MLA_HEADS):
            acc = acc_sc[h]
            o_ref[0, :, h * V_DIM:(h + 1) * V_DIM] = (acc[:, :V_DIM] / acc[:, V_DIM:V_DIM + 1]).astype(BF16)


def flash_attention(q, k, v, blk=512):
    b, nh, s, _ = q.shape
    blk = min(blk, s)
    nq = s // blk
    qi = np.concatenate([np.full(i + 1, i, np.int32) for i in range(nq)])
    ki = np.concatenate([np.arange(i + 1, dtype=np.int32) for i in range(nq)])
    grid_spec = pltpu.PrefetchScalarGridSpec(
        num_scalar_prefetch=2,
        grid=(b, len(qi)),
        in_specs=[pl.BlockSpec((1, nh, blk, QK_PAD), lambda bi, p, qi_r, ki_r: (bi, 0, qi_r[p], 0)),
                  pl.BlockSpec((1, nh, QK_PAD, blk), lambda bi, p, qi_r, ki_r: (bi, 0, 0, ki_r[p])),
                  pl.BlockSpec((1, nh, blk, V_PAD), lambda bi, p, qi_r, ki_r: (bi, 0, ki_r[p], 0))],
        out_specs=pl.BlockSpec((1, blk, nh * V_DIM), lambda bi, p, qi_r, ki_r: (bi, qi_r[p], 0)),
        scratch_shapes=[pltpu.VMEM((nh, blk, LANES), F32),
                        pltpu.VMEM((nh, blk, V_PAD), F32)],
    )
    return pl.pallas_call(
        functools.partial(_flash_kernel, blk=blk),
        grid_spec=grid_spec,
        out_shape=jax.ShapeDtypeStruct((b, s, nh * V_DIM), BF16),
        compiler_params=_params("arbitrary", "arbitrary"),
        name="mla_attention",
    )(jnp.asarray(qi), jnp.asarray(ki), q, k, v)


def _sorting_network(n):
    pairs = []
    p = 1
    while p < n:
        k = p
        while k >= 1:
            for j in range(k % p, n - k, 2 * k):
                for i in range(min(k, n - j - k)):
                    if (i + j) // (2 * p) == (i + j + k) // (2 * p):
                        pairs.append((i + j, i + j + k))
            k //= 2
        p *= 2
    return pairs


def _largest16(tiles, only_last):
    n = PEER_TOPK
    v = list(tiles)

    def exchange(i, j):
        v[i], v[j] = jnp.maximum(v[i], v[j]), jnp.minimum(v[i], v[j])

    for i, j in _sorting_network(n):
        exchange(i, j)
    shift = v[0].shape[0] // 2
    while shift >= 1:
        other = [pltpu.roll(x, shift, axis=0) for x in v]
        v = [jnp.maximum(v[i], other[n - 1 - i]) for i in range(n)]
        if only_last and shift == 1:
            return functools.reduce(jnp.minimum, v)
        stride = n // 2
        while stride >= 1:
            for i in range(n):
                if (i // stride) % 2 == 0:
                    exchange(i, i + stride)
            stride //= 2
        shift //= 2
    return v


def _top16(e):
    sub_rows = e.shape[0] // PEER_TOPK
    v = _largest16([e[r * sub_rows:(r + 1) * sub_rows] for r in range(PEER_TOPK)], only_last=False)
    sub = lax.broadcasted_iota(jnp.int32, v[0].shape, 0)
    halves = []
    for base in range(0, PEER_TOPK, sub_rows):
        rows = v[base]
        for r in range(1, sub_rows):
            rows = jnp.where(sub == r, v[base + r], rows)
        halves.append(rows)
    return jnp.concatenate(halves, axis=0)


def _pair_candidates(ea, eb):
    parts = [ea[0:1] * eb]
    parts += [ea[p:p + 1] * eb[0:8] for p in range(1, 8)]
    parts += [ea[8:16] * eb[0:1]]
    return jnp.concatenate(parts, axis=0)


def _router_kernel(x_ref, g_ref, wq_ref, keys_ref, xn_ref, e1_ref, e2_ref, tau_ref):
    xn_f32 = _rms(x_ref[...], g_ref[...])
    xn_ref[...] = xn_f32.T.astype(BF16)
    q = _dot(xn_f32.astype(BF16), wq_ref[...]).astype(BF16)
    for h in range(PEER_HEADS):
        es, tops = [], []
        for half in range(2):
            j = 2 * h + half
            s_t = _dot_nt(keys_ref[j], q[:, j * PEER_HALF:(j + 1) * PEER_HALF])
            e = jnp.exp(s_t - jnp.max(s_t, axis=0, keepdims=True))
            es.append(e)
            tops.append(_top16(e))
        ea, eb = tops
        cand = _pair_candidates(ea, eb)
        n_tiles = cand.shape[0] // 8
        tiles = [cand[8 * r:8 * (r + 1)] for r in range(n_tiles)]
        tiles += [jnp.zeros_like(tiles[0])] * (PEER_TOPK - n_tiles)
        kth = _largest16(tiles, only_last=True)[0:1]
        sel = cand >= jnp.maximum(kth, 1e-30)
        z = jnp.sum(jnp.where(sel, cand, 0.0), axis=0, keepdims=True)
        scale = 0.5 / z
        cand_z = _pair_candidates(ea * scale, eb)
        tau_ref[h:h + 1, :] = jnp.min(jnp.where(sel, cand_z, jnp.inf), axis=0, keepdims=True)
        e1_ref[h] = es[0] * scale
        e2_ref[h] = es[1]


def peer_router(x, gain, wq, keys, tr=256):
    n, d = x.shape
    tr = min(tr, n)
    return pl.pallas_call(
        _router_kernel,
        grid=(n // tr,),
        in_specs=[pl.BlockSpec((tr, d), lambda i: (i, 0)),
                  pl.BlockSpec((1, d), lambda i: (0, 0)),
                  pl.BlockSpec(wq.shape, lambda i: (0, 0)),
                  pl.BlockSpec(keys.shape, lambda i: (0, 0, 0))],
        out_specs=[pl.BlockSpec((d, tr), lambda i: (0, i)),
                   pl.BlockSpec((PEER_HEADS, N_KEYS, tr), lambda i: (0, 0, i)),
                   pl.BlockSpec((PEER_HEADS, N_KEYS, tr), lambda i: (0, 0, i)),
                   pl.BlockSpec((PEER_HEADS, tr), lambda i: (0, i))],
        out_shape=[jax.ShapeDtypeStruct((d, n), BF16),
                   jax.ShapeDtypeStruct((PEER_HEADS, N_KEYS, n), F32),
                   jax.ShapeDtypeStruct((PEER_HEADS, N_KEYS, n), F32),
                   jax.ShapeDtypeStruct((PEER_HEADS, n), F32)],
        compiler_params=_params("parallel"),
        name="peer_router",
    )(x, gain.reshape(1, d), wq, keys)


def _peer_expert_kernel(xn_ref, x_ref, e1_ref, e2_ref, tau_ref, u_ref, v_ref, out_gain_ref, o_ref,
                        acc_sc, act_sc, w_sc, *, et, n_sub, tc, norm_output):
    e = pl.program_id(1)
    tb = xn_ref.shape[1]
    n_i = et // N_KEYS

    @pl.when(e == 0)
    def _():
        acc_sc[...] = jnp.zeros(acc_sc.shape, F32)

    mxu_w = 2 * LANES

    def rows(s):
        return slice(s * et, (s + 1) * et)

    def act_pieces(s):
        def piece(c):
            cols = slice(c * mxu_w, (c + 1) * mxu_w)
            act_sc[s, :, cols] = _dot(u_ref[rows(s), :], xn_ref[:, cols])
        return [functools.partial(piece, c) for c in range(tb // mxu_w)]

    def out_pieces(s):
        def piece(c):
            cols = slice(c * mxu_w, (c + 1) * mxu_w)
            acc_sc[:, cols] += _dot(v_ref[:, rows(s)], w_sc[s, :, cols])
        return [functools.partial(piece, c) for c in range(tb // mxu_w)]

    def build_pieces(s):
        def piece(i, c):
            ig = (e * n_sub + s) * n_i + i
            r = slice(i * N_KEYS, (i + 1) * N_KEYS)
            cols = slice(c * tc, (c + 1) * tc)

            gates = jnp.zeros((N_KEYS, tc), F32)
            for h in range(PEER_HEADS):
                p = e2_ref[h, :, cols] * e1_ref[h, pl.ds(ig, 1), :][:, cols]
                gates = gates + jnp.where(p >= tau_ref[h:h + 1, cols], p, 0.0)
            a = act_sc[s, r, cols]
            two_gelu = a * (1.0 + jnp.tanh(a * (GELU_C1 + GELU_C3 * (a * a))))
            w_sc[s, r, cols] = (gates * two_gelu).astype(BF16)
        return [functools.partial(piece, i, c) for i in range(n_i) for c in range(tb // tc)]

    def interleave(vector_work, matrix_work):
        done = 0
        for k, piece in enumerate(vector_work):
            piece()
            due = (k + 1) * len(matrix_work) // len(vector_work)
            for m in matrix_work[done:due]:
                m()
            done = due

    for m in act_pieces(0):
        m()
    for s in range(n_sub):
        matrix_work = (act_pieces(s + 1) if s + 1 < n_sub else []) + (out_pieces(s - 1) if s > 0 else [])
        interleave(build_pieces(s), matrix_work)
    for m in out_pieces(n_sub - 1):
        m()

    @pl.when(e == pl.num_programs(1) - 1)
    def _():
        y = x_ref[...] + acc_sc[...].T
        o_ref[...] = _rms(y, out_gain_ref[...]) if norm_output else y


def peer_experts(xn, x, e1, e2, tau, u, v, out_gain, norm_output, tb=512, et=512, n_sub=4, tc=128):
    n, d = x.shape
    ne = u.shape[0]
    tb = min(tb, n)
    tc = min(tc, tb)
    blk_e = et * n_sub
    per_key = pl.BlockSpec((PEER_HEADS, N_KEYS, tb), lambda i, j: (0, 0, i))
    return pl.pallas_call(
        functools.partial(_peer_expert_kernel, et=et, n_sub=n_sub, tc=tc, norm_output=norm_output),
        grid=(n // tb, ne // blk_e),
        in_specs=[pl.BlockSpec((d, tb), lambda i, j: (0, i)),
                  pl.BlockSpec((tb, d), lambda i, j: (i, 0)),
                  per_key, per_key, pl.BlockSpec((PEER_HEADS, tb), lambda i, j: (0, i)),
                  pl.BlockSpec((blk_e, d), lambda i, j: (j, 0)),
                  pl.BlockSpec((d, blk_e), lambda i, j: (0, j)),
                  pl.BlockSpec((1, d), lambda i, j: (0, 0))],
        out_specs=pl.BlockSpec((tb, d), lambda i, j: (i, 0)),
        out_shape=jax.ShapeDtypeStruct((n, d), F32),
        scratch_shapes=[pltpu.VMEM((d, tb), F32),
                        pltpu.VMEM((n_sub, et, tb), F32),
                        pltpu.VMEM((n_sub, et, tb), BF16)],
        compiler_params=_params("parallel", "arbitrary"),
        name="peer_experts",
    )(xn, x, e1, e2, tau, u, v, out_gain.reshape(1, d))


def _swap_halves(w):
    half = w.shape[-1] // 2
    return jnp.concatenate([w[..., half:], w[..., :half]], axis=-1)


def _even_layer(x, gain, w_in, v_gain, ws, sb, conv_w, conv_b, i_bias, f_bias, h_gain, w_out):
    b, s, d = x.shape
    w_in_p = jnp.pad(w_in, ((0, 0), (0, EV_IN_PAD - EV_IN))).astype(BF16)
    proj = norm_matmul(x.reshape(b * s, d), gain, w_in_p).reshape(b, s, EV_IN_PAD)
    gbias = jnp.pad(jnp.concatenate([i_bias, f_bias]), (0, LANES - 2 * ML_HEADS)).reshape(1, LANES)
    return even_mixer(proj, x, v_gain.reshape(1, GM_W), ws, sb.T, conv_w, conv_b.reshape(1, 2 * ML_W),
                      gbias, h_gain.reshape(1, ML_W), w_out.astype(BF16))


def _odd_layer(x, gain, w_in, q_gain, kv_gain, w_uq, w_ukv, w_out, cs):
    b, s, d = x.shape
    rope_cols = w_in[:, Q_LORA + KV_LORA:]
    win = jnp.concatenate([w_in, _swap_halves(rope_cols)], axis=1).astype(BF16)
    wq = w_uq.reshape(Q_LORA, MLA_HEADS, QK_NOPE + QK_ROPE)
    wuq = jnp.concatenate([wq, _swap_halves(wq[..., QK_NOPE:])], axis=-1)
    wuq = wuq.reshape(Q_LORA, MLA_HEADS * QK_PAD).astype(BF16)
    q, k, v = mla_prep(x, gain.reshape(1, d), win, q_gain.reshape(1, Q_LORA), kv_gain.reshape(1, KV_LORA),
                       wuq, w_ukv.astype(BF16), cs)
    o = flash_attention(q, k, v)
    return matmul_residual(o.reshape(b * s, MLA_HEADS * V_DIM), w_out.astype(BF16),
                           x.reshape(b * s, d)).reshape(b, s, d)


def _peer_layer(x, gain, w_q, keys, u_tab, v_tab, out_gain, norm_output):
    b, s, d = x.shape
    x2 = x.reshape(b * s, d)
    keys16 = keys.reshape(PEER_HEADS * 2, N_KEYS, PEER_HALF).astype(BF16)
    xn, e1, e2, tau = peer_router(x2, gain, w_q.astype(BF16), keys16)
    out = peer_experts(xn, x2, e1, e2, tau, u_tab.astype(BF16), v_tab.astype(BF16).T, out_gain, norm_output)
    return out.reshape(b, s, d)


def kernel(x, norm_mix, norm_ffn, norm_final, ev_w_in, ev_gm_v_gain, ev_gm_ws, ev_gm_b, ev_conv_w, ev_conv_b, ev_i_bias, ev_f_bias, ev_h_gain, ev_w_out, od_w_in, od_q_gain, od_kv_gain, od_w_uq, od_w_ukv, od_w_out, peer_w_q, peer_keys, peer_u, peer_v):
    b, s, d = x.shape
    depth = norm_mix.shape[0]
    pos = jnp.arange(s, dtype=F32)
    inv_freq = ROPE_THETA ** (-jnp.arange(QK_ROPE // 2, dtype=F32) / (QK_ROPE // 2))
    ang = pos[:, None] * inv_freq[None, :]
    cos, sin = jnp.cos(ang), jnp.sin(ang)
    cs = jnp.concatenate([cos, cos, -sin, sin], axis=1)
    for layer in range(depth):
        j = layer // 2
        if layer % 2 == 0:
            x = _even_layer(x, norm_mix[layer], ev_w_in[j], ev_gm_v_gain[j], ev_gm_ws[j], ev_gm_b[j],
                            ev_conv_w[j], ev_conv_b[j], ev_i_bias[j], ev_f_bias[j], ev_h_gain[j], ev_w_out[j])
        else:
            x = _odd_layer(x, norm_mix[layer], od_w_in[j], od_q_gain[j], od_kv_gain[j], od_w_uq[j],
                           od_w_ukv[j], od_w_out[j], cs)
        x = _peer_layer(x, norm_ffn[layer], peer_w_q[layer], peer_keys[layer], peer_u[layer], peer_v[layer],
                        norm_final, norm_output=(layer == depth - 1))
    return x
```
